```python
import math
import jax, jax.numpy as jnp
from jax import lax
import numpy as np

D_MODEL = 4096
BATCH = 4
SEQ = 2048
DEPTH = 4

M_HEADS = 6
M_DK = 128
M_DV = 256
DN_HEADS = 12
DN_DK = 128
DN_DV = 128
DN_CONV = 4
G_HEADS = 6
G_DK = 128
G_DV = 256
G_RANK = 16
G_TAU = 16.0
CHUNK = 64
G_CHUNK = 16
N_BRANCH = 3
BRANCH_W = 1536
D_FF = 6144
FFN_CONV = 3
EPS = 1e-6

IN_SIZES = (
    M_HEADS * M_DK, M_HEADS * M_DK, M_HEADS * M_DV, M_HEADS * M_DV, M_HEADS, M_HEADS,
    DN_HEADS * DN_DK, DN_HEADS * DN_DK, DN_HEADS * DN_DV, DN_HEADS * DN_DV, DN_HEADS, DN_HEADS,
    G_HEADS * G_DK, G_HEADS * G_DK, G_HEADS * G_DV, G_HEADS * G_DV, G_RANK,
    N_BRANCH * D_MODEL,
)
D_IN = sum(IN_SIZES)

kernel_name = "hybrid_mlstm_gdn_gla_convffn"


def rms_norm(x, w):
    xf = x.astype(jnp.float32)
    y = xf * lax.rsqrt(jnp.mean(xf * xf, axis=-1, keepdims=True) + EPS)
    return (y * w.astype(jnp.float32)).astype(x.dtype)


def l2norm(x):
    xf = x.astype(jnp.float32)
    return xf * lax.rsqrt(jnp.sum(xf * xf, axis=-1, keepdims=True) + EPS)


def causal_dwconv(x, w):
    k_w = w.shape[0]
    seq = x.shape[1]
    xp = jnp.pad(x, ((0, 0), (k_w - 1, 0), (0, 0)))
    y = w[0] * xp[:, 0:seq]
    for j in range(1, k_w):
        y = y + w[j] * xp[:, j:j + seq]
    return y


def _heads(t, n):
    return t.reshape(t.shape[0], t.shape[1], n, -1)


def to_chunks(x, length):
    b, s, h = x.shape[:3]
    x = x.reshape((b, s // length, length, h) + x.shape[3:])
    return jnp.moveaxis(x, (1, 3), (0, 2))


def from_chunks(y):
    n, b, h, l = y.shape[:4]
    y = jnp.moveaxis(y, (0, 2), (1, 3))
    return y.reshape((b, n * l, h) + y.shape[4:])


def mlstm_mixer(q, k, v, i_pre, log_f):
    f32 = jnp.float32
    bsz, _, nh, dk = q.shape
    dv = v.shape[-1]
    qc = to_chunks(q.astype(f32) * dk ** -0.5, CHUNK)
    kc = to_chunks(k.astype(f32), CHUNK)
    vc = to_chunks(v.astype(f32), CHUNK)
    ic = to_chunks(i_pre.astype(f32), CHUNK)
    bc = jnp.cumsum(to_chunks(log_f.astype(f32), CHUNK), axis=-1)
    causal = jnp.tril(jnp.ones((CHUNK, CHUNK), dtype=bool))

    def step(carry, inp):
        c_prev, n_prev, m_prev = carry
        qj, kj, vj, ij, bj = inp
        log_d = jnp.where(causal, bj[..., :, None] - bj[..., None, :] + ij[..., None, :], -jnp.inf)
        m_inter = bj + m_prev[..., None]
        m_t = jnp.maximum(m_inter, jnp.max(log_d, axis=-1))
        s = jnp.einsum('bhtk,bhsk->bhts', qj, kj) * jnp.exp(log_d - m_t[..., None])
        w_inter = jnp.exp(m_inter - m_t)
        num = (w_inter[..., None] * jnp.einsum('bhtk,bhkv->bhtv', qj, c_prev)
               + jnp.einsum('bhts,bhsv->bhtv', s, vj))
        den = w_inter * jnp.einsum('bhtk,bhk->bht', qj, n_prev) + jnp.sum(s, axis=-1)
        h = num / jnp.maximum(jnp.abs(den), jnp.exp(-m_t))[..., None]
        b_end = bj[..., -1]
        log_w = b_end[..., None] - bj + ij
        m_new = jnp.maximum(b_end + m_prev, jnp.max(log_w, axis=-1))
        w = jnp.exp(log_w - m_new[..., None])
        carry_decay = jnp.exp(b_end + m_prev - m_new)
        c_new = carry_decay[..., None, None] * c_prev + jnp.einsum('bhs,bhsk,bhsv->bhkv', w, kj, vj)
        n_new = carry_decay[..., None] * n_prev + jnp.einsum('bhs,bhsk->bhk', w, kj)
        return (c_new, n_new, m_new), h

    init = (jnp.zeros((bsz, nh, dk, dv), f32), jnp.zeros((bsz, nh, dk), f32), jnp.zeros((bsz, nh), f32))
    _, h = lax.scan(step, init, (qc, kc, vc, ic, bc))
    return from_chunks(h)


def gated_delta_mixer(q, k, v, beta, g):
    f32 = jnp.float32
    bsz, _, nh, dk = q.shape
    dv = v.shape[-1]
    qc = to_chunks(q.astype(f32) * dk ** -0.5, CHUNK)
    kc = to_chunks(k.astype(f32), CHUNK)
    vc = to_chunks(v.astype(f32), CHUNK)
    bc = to_chunks(beta.astype(f32), CHUNK)
    gc = jnp.cumsum(to_chunks(g.astype(f32), CHUNK), axis=-1)
    incl = jnp.tril(jnp.ones((CHUNK, CHUNK), dtype=bool))
    strict = jnp.tril(jnp.ones((CHUNK, CHUNK), dtype=bool), k=-1)
    decay = jnp.exp(jnp.where(incl, gc[..., :, None] - gc[..., None, :], -jnp.inf))
    a = jnp.einsum('nbhtk,nbhsk->nbhts', kc * bc[..., None], kc) * jnp.where(strict, decay, 0.0)
    rhs = jnp.concatenate([kc * (bc * jnp.exp(gc))[..., None], vc * bc[..., None]], axis=-1)
    sol = lax.linalg.triangular_solve(a + jnp.eye(CHUNK, dtype=f32), rhs,
                                      left_side=True, lower=True, unit_diagonal=True)
    w, u = sol[..., :dk], sol[..., dk:]
    p = jnp.einsum('nbhtk,nbhsk->nbhts', qc, kc) * decay
    qg = qc * jnp.exp(gc)[..., None]
    g_end = gc[..., -1]
    kd = kc * jnp.exp(g_end[..., None] - gc)[..., None]

    def step(s_prev, inp):
        qgj, pj, kdj, wj, uj, gej = inp
        u_new = uj - jnp.einsum('bhtk,bhkv->bhtv', wj, s_prev)
        o = jnp.einsum('bhtk,bhkv->bhtv', qgj, s_prev) + jnp.einsum('bhts,bhsv->bhtv', pj, u_new)
        s_new = jnp.exp(gej)[..., None, None] * s_prev + jnp.einsum('bhsk,bhsv->bhkv', kdj, u_new)
        return s_new, o

    _, o = lax.scan(step, jnp.zeros((bsz, nh, dk, dv), f32), (qg, p, kd, w, u, g_end))
    return from_chunks(o)


def gla_mixer(q, k, v, log_a):
    f32 = jnp.float32
    bsz, _, nh, dk = q.shape
    dv = v.shape[-1]
    qc = to_chunks(q.astype(f32) * dk ** -0.5, G_CHUNK)
    kc = to_chunks(k.astype(f32), G_CHUNK)
    vc = to_chunks(v.astype(f32), G_CHUNK)
    lc = jnp.cumsum(to_chunks(log_a.astype(f32), G_CHUNK), axis=-2)
    l_end = lc[..., -1:, :]
    q_in = qc * jnp.exp(lc)
    k_in = kc * jnp.exp(-lc)
    kd = kc * jnp.exp(l_end - lc)
    incl = jnp.tril(jnp.ones((G_CHUNK, G_CHUNK), dtype=bool))
    p = jnp.where(incl, jnp.einsum('nbhtk,nbhsk->nbhts', q_in, k_in), 0.0)
    o_intra = jnp.einsum('nbhts,nbhsv->nbhtv', p, vc)

    def step(s_prev, inp):
        qj, kdj, vj, lej = inp
        o = jnp.einsum('bhtk,bhkv->bhtv', qj, s_prev)
        s_new = jnp.exp(lej)[..., 0, :, None] * s_prev + jnp.einsum('bhsk,bhsv->bhkv', kdj, vj)
        return s_new, o

    _, o_inter = lax.scan(step, jnp.zeros((bsz, nh, dk, dv), f32), (q_in, kd, vc, l_end))
    return from_chunks(o_inter + o_intra)


def hybrid_layer(x, g_mix_pre, g_mix_post, g_ffn_pre, g_ffn_post, w_in, m_gate_b, m_norm,
                 dn_conv, dn_a_log, dn_dt_bias, dn_norm, gla_w_a2, gla_b_a, gla_norm,
                 w_branch, w_out, w_ffn_up, ffn_conv, w_ffn_down):
    f32 = jnp.float32
    bsz, seq, _ = x.shape
    dt = x.dtype
    h = rms_norm(x, g_mix_pre)
    splits = np.cumsum(IN_SIZES)[:-1].tolist()
    (m_q, m_k, m_v, m_o, m_i, m_f,
     d_q, d_k, d_v, d_z, d_b, d_a,
     g_q, g_k, g_v, g_r, g_lr, merge_pre) = jnp.split(h @ w_in, splits, axis=-1)

    i_pre = m_i.astype(f32) + m_gate_b[:M_HEADS].astype(f32)
    log_f = jax.nn.log_sigmoid(m_f.astype(f32) + m_gate_b[M_HEADS:].astype(f32))
    h_m = mlstm_mixer(_heads(m_q, M_HEADS), _heads(m_k, M_HEADS), _heads(m_v, M_HEADS), i_pre, log_f)
    y_m = jax.nn.sigmoid(m_o) * rms_norm(h_m, m_norm).reshape(bsz, seq, BRANCH_W).astype(dt)

    qkv = jax.nn.silu(causal_dwconv(jnp.concatenate([d_q, d_k, d_v], axis=-1), dn_conv))
    d_q, d_k, d_v = jnp.split(qkv, [DN_HEADS * DN_DK, 2 * DN_HEADS * DN_DK], axis=-1)
    beta = jax.nn.sigmoid(d_b.astype(f32))
    g = -jnp.exp(dn_a_log.astype(f32)) * jax.nn.softplus(d_a.astype(f32) + dn_dt_bias.astype(f32))
    h_d = gated_delta_mixer(l2norm(_heads(d_q, DN_HEADS)), l2norm(_heads(d_k, DN_HEADS)),
                            _heads(d_v, DN_HEADS), beta, g)
    y_d = (rms_norm(h_d, dn_norm) * jax.nn.silu(_heads(d_z, DN_HEADS).astype(f32))
           ).reshape(bsz, seq, BRANCH_W).astype(dt)

    log_a = jax.nn.log_sigmoid((g_lr @ gla_w_a2 + gla_b_a).astype(f32)) / G_TAU
    h_g = gla_mixer(_heads(g_q, G_HEADS), _heads(g_k, G_HEADS), _heads(g_v, G_HEADS),
                    _heads(log_a, G_HEADS))
    y_g = (rms_norm(h_g, gla_norm) * jax.nn.silu(_heads(g_r, G_HEADS).astype(f32))
           ).reshape(bsz, seq, BRANCH_W).astype(dt)

    ys = jnp.stack([y_m, y_d, y_g], axis=2)
    gates = jax.nn.sigmoid(merge_pre.reshape(bsz, seq, N_BRANCH, D_MODEL))
    merged = jnp.sum(gates * jnp.einsum('bsgc,gcd->bsgd', ys, w_branch), axis=2)
    x = x + rms_norm(merged @ w_out, g_mix_post)

    h = rms_norm(x, g_ffn_pre)
    gate, up = jnp.split(causal_dwconv(h @ w_ffn_up, ffn_conv), 2, axis=-1)
    x = x + rms_norm((jax.nn.silu(gate) * up) @ w_ffn_down, g_ffn_post)
    return x


def setup_inputs(seed: int = 0) -> dict:
    key = jax.random.key(seed)
    ks = jax.random.split(key, 24)
    f32 = jnp.float32

    def nrm(k, shape, scale):
        return jax.random.normal(k, shape, f32) * scale

    def gain(k, shape):
        return 1.0 + 0.05 * jax.random.normal(k, shape, f32)

    conv_ch = 2 * DN_HEADS * DN_DK + DN_HEADS * DN_DV
    dt_init = jnp.exp(jax.random.uniform(ks[9], (DEPTH, DN_HEADS), f32, math.log(1e-3), math.log(1e-1)))
    m_gate_b = jnp.concatenate([
        nrm(ks[6], (DEPTH, M_HEADS), 0.1),
        3.0 + nrm(ks[7], (DEPTH, M_HEADS), 0.5)], axis=-1)
    return {
        "x": nrm(ks[0], (BATCH, SEQ, D_MODEL), 1.0),
        "norm_mix_pre": gain(ks[1], (DEPTH, D_MODEL)),
        "norm_mix_post": gain(ks[2], (DEPTH, D_MODEL)),
        "norm_ffn_pre": gain(ks[3], (DEPTH, D_MODEL)),
        "norm_ffn_post": gain(ks[4], (DEPTH, D_MODEL)),
        "w_in": nrm(ks[5], (DEPTH, D_MODEL, D_IN), D_MODEL ** -0.5),
        "mlstm_gate_b": m_gate_b,
        "mlstm_norm": gain(ks[8], (DEPTH, M_HEADS, M_DV)),
        "dn_conv": nrm(ks[10], (DEPTH, DN_CONV, conv_ch), DN_CONV ** -0.5),
        "dn_a_log": jnp.log(jax.random.uniform(ks[11], (DEPTH, DN_HEADS), f32, 1.0, 16.0)),
        "dn_dt_bias": dt_init + jnp.log(-jnp.expm1(-dt_init)),
        "dn_norm": gain(ks[12], (DEPTH, DN_DV)),
        "gla_w_a2": nrm(ks[13], (DEPTH, G_RANK, G_HEADS * G_DK), G_RANK ** -0.5),
        "gla_b_a": nrm(ks[14], (DEPTH, G_HEADS * G_DK), 0.1),
        "gla_norm": gain(ks[15], (DEPTH, G_DV)),
        "w_branch": nrm(ks[16], (DEPTH, N_BRANCH, BRANCH_W, D_MODEL), BRANCH_W ** -0.5),
        "w_out": nrm(ks[17], (DEPTH, D_MODEL, D_MODEL), D_MODEL ** -0.5),
        "w_ffn_up": nrm(ks[18], (DEPTH, D_MODEL, 2 * D_FF), D_MODEL ** -0.5),
        "ffn_conv": nrm(ks[19], (DEPTH, FFN_CONV, 2 * D_FF), FFN_CONV ** -0.5),
        "w_ffn_down": nrm(ks[20], (DEPTH, D_FF, D_MODEL), D_FF ** -0.5),
    }


def reference(x, norm_mix_pre, norm_mix_post, norm_ffn_pre, norm_ffn_post, w_in, mlstm_gate_b,
              mlstm_norm, dn_conv, dn_a_log, dn_dt_bias, dn_norm, gla_w_a2, gla_b_a, gla_norm,
              w_branch, w_out, w_ffn_up, ffn_conv, w_ffn_down):
    for l in range(DEPTH):
        x = hybrid_layer(x, norm_mix_pre[l], norm_mix_post[l], norm_ffn_pre[l], norm_ffn_post[l],
                         w_in[l], mlstm_gate_b[l], mlstm_norm[l], dn_conv[l], dn_a_log[l],
                         dn_dt_bias[l], dn_norm[l], gla_w_a2[l], gla_b_a[l], gla_norm[l],
                         w_branch[l], w_out[l], w_ffn_up[l], ffn_conv[l], w_ffn_down[l])
    return x
```

```python
import functools

import numpy as np
import jax
import jax.numpy as jnp
from jax import lax
from jax.experimental import pallas as pl
from jax.experimental.pallas import tpu as pltpu

F32 = jnp.float32
BF16 = jnp.bfloat16
EPS = 1e-6
LANE = 128
SUBLANE = 8
VMEM_LIMIT = 56 * 1024 * 1024

M_HEADS, M_DK, M_DV = 6, 128, 256
DN_HEADS, DN_DK, DN_DV, DN_CONV = 12, 128, 128, 4
G_HEADS, G_DK, G_DV, G_RANK, G_TAU = 6, 128, 256, 16, 16.0
CHUNK, G_CHUNK = 64, 16
N_BRANCH, BRANCH_W, FFN_CONV = 3, 1536, 3

M_W = 2 * M_HEADS * M_DK + 2 * M_HEADS * M_DV
D_W = 2 * DN_HEADS * DN_DK + 2 * DN_HEADS * DN_DV
G_W = 2 * G_HEADS * G_DK + 2 * G_HEADS * G_DV
OFF_M = 0
OFF_MGATE = OFF_M + M_W
OFF_D = OFF_MGATE + 2 * M_HEADS
OFF_DGATE = OFF_D + D_W
OFF_G = OFF_DGATE + 2 * DN_HEADS
OFF_GLR = OFF_G + G_W
OFF_MERGE = OFF_GLR + G_RANK
LN_MI = OFF_MGATE % LANE
LN_MF = LN_MI + M_HEADS
LN_DB = OFF_DGATE % LANE
LN_DA = LN_DB + DN_HEADS
LN_GLR = OFF_GLR % LANE
LN_END = LN_GLR + G_RANK
assert LN_MI == 0 and LN_MF <= LN_DB and LN_DA + DN_HEADS <= LN_GLR and LN_END <= LANE


def _cparams(*sem):
    return pltpu.CompilerParams(dimension_semantics=sem, vmem_limit_bytes=VMEM_LIMIT)


def _dot(a, b):
    return jnp.dot(a, b, preferred_element_type=F32)


def _dot_nt(a, b):
    return lax.dot_general(a, b, (((1,), (1,)), ((), ())), preferred_element_type=F32)


def _dot_tn(a, b):
    return lax.dot_general(a, b, (((0,), (0,)), ((), ())), preferred_element_type=F32)


def _dot_f32(a, b):
    return jnp.dot(a, b, preferred_element_type=F32, precision=lax.Precision.HIGHEST)


def _sigmoid(x):
    return 1.0 / (1.0 + jnp.exp(-x))


def _silu(x):
    return x * _sigmoid(x)


def _log_sigmoid(x):
    return jnp.minimum(x, 0.0) - jnp.log1p(jnp.exp(-jnp.abs(x)))


def _softplus(x):
    return jnp.maximum(x, 0.0) + jnp.log1p(jnp.exp(-jnp.abs(x)))


def _iota2(shape, axis):
    return lax.broadcasted_iota(jnp.int32, shape, axis)


def _tile(n, pref):
    t = min(n, pref)
    assert n % t == 0, (n, pref)
    return t


def _norm_cast_kernel(x_ref, g_ref, o_ref):
    x = x_ref[...]
    y = x * lax.rsqrt(jnp.mean(x * x, axis=-1, keepdims=True) + EPS)
    o_ref[...] = (y * g_ref[...]).astype(o_ref.dtype)


def _norm_cast(x2, g):
    t, d = x2.shape
    tr = _tile(t, 256)
    return pl.pallas_call(
        _norm_cast_kernel,
        grid=(t // tr,),
        in_specs=[pl.BlockSpec((tr, d), lambda i: (i, 0)), pl.BlockSpec((1, d), lambda i: (0, 0))],
        out_specs=pl.BlockSpec((tr, d), lambda i: (i, 0)),
        out_shape=jax.ShapeDtypeStruct((t, d), BF16),
        compiler_params=_cparams("parallel"),
        name="norm_cast",
    )(x2, g.reshape(1, d))


def _resid_norm_kernel(x_ref, y_ref, gp_ref, gn_ref, xo_ref, ho_ref):
    y = y_ref[...]
    yn = y * lax.rsqrt(jnp.mean(y * y, axis=-1, keepdims=True) + EPS) * gp_ref[...]
    xn = x_ref[...] + yn
    xo_ref[...] = xn
    hn = xn * lax.rsqrt(jnp.mean(xn * xn, axis=-1, keepdims=True) + EPS)
    ho_ref[...] = (hn * gn_ref[...]).astype(ho_ref.dtype)


def _resid_kernel(x_ref, y_ref, gp_ref, xo_ref):
    y = y_ref[...]
    yn = y * lax.rsqrt(jnp.mean(y * y, axis=-1, keepdims=True) + EPS) * gp_ref[...]
    xo_ref[...] = x_ref[...] + yn


def _resid_norm(x2, y, g_post, g_next):
    t, d = x2.shape
    tr = _tile(t, 256)
    row = pl.BlockSpec((tr, d), lambda i: (i, 0))
    vec = pl.BlockSpec((1, d), lambda i: (0, 0))
    if g_next is None:
        return pl.pallas_call(
            _resid_kernel,
            grid=(t // tr,),
            in_specs=[row, row, vec],
            out_specs=row,
            out_shape=jax.ShapeDtypeStruct((t, d), F32),
            compiler_params=_cparams("parallel"),
            name="resid",
        )(x2, y, g_post.reshape(1, d)), None
    return pl.pallas_call(
        _resid_norm_kernel,
        grid=(t // tr,),
        in_specs=[row, row, vec, vec],
        out_specs=[row, row],
        out_shape=[jax.ShapeDtypeStruct((t, d), F32), jax.ShapeDtypeStruct((t, d), BF16)],
        compiler_params=_cparams("parallel"),
        name="resid_norm",
    )(x2, y, g_post.reshape(1, d), g_next.reshape(1, d))


def _mm_kernel(h_ref, w_ref, o_ref):
    o_ref[...] = _dot(h_ref[...], w_ref[...].astype(BF16)).astype(o_ref.dtype)


def _mm_shift_kernel(h_ref, wm_ref, wn_ref, o_ref, *, shift):
    w = jnp.concatenate([wm_ref[:, shift:], wn_ref[:, :shift]], axis=1)
    o_ref[...] = _dot(h_ref[...], w.astype(BF16)).astype(o_ref.dtype)


def _matmul(h, w3, layer, col0, n, out_dtype, *, tm=1024, tn=512, name="matmul"):
    t, k = h.shape
    tm = _tile(t, tm)
    tn = _tile(n, tn)
    shift = col0 % LANE
    base = col0 - shift
    assert base % tn == 0 and tn % LANE == 0
    cb = base // tn
    lane_blocks = tn // LANE
    in_specs = [pl.BlockSpec((tm, k), lambda i, j: (i, 0)), pl.BlockSpec((None, k, tn), lambda i, j: (layer, 0, cb + j))]
    if shift:
        body = functools.partial(_mm_shift_kernel, shift=shift)
        in_specs.append(pl.BlockSpec((None, k, LANE), lambda i, j: (layer, 0, (cb + j + 1) * lane_blocks)))
    else:
        body = _mm_kernel
    return pl.pallas_call(
        body,
        grid=(t // tm, n // tn),
        in_specs=in_specs,
        out_specs=pl.BlockSpec((tm, tn), lambda i, j: (i, j)),
        out_shape=jax.ShapeDtypeStruct((t, n), out_dtype),
        compiler_params=_cparams("parallel", "arbitrary"),
        name=name,
    )(*([h, w3, w3] if shift else [h, w3]))


def _gates_kernel(h_ref, wa_ref, wb_ref, wc_ref, g_ref, gt_ref):
    h = h_ref[...]
    pa = _dot(h, wa_ref[...].astype(BF16))
    pb = _dot(h, wb_ref[...].astype(BF16))
    pc = _dot(h, wc_ref[...].astype(BF16))
    lane = _iota2(pa.shape, 1)
    g = jnp.where(lane < LN_DB, pa, jnp.where(lane < LN_GLR, pb, pc))
    g = jnp.where(lane < LN_END, g, 0.0)
    g_ref[...] = g
    gt_ref[...] = g.T


def _gates(h, w_in, layer):
    t, k = h.shape
    tm = _tile(t, 512)
    wspec = lambda col: pl.BlockSpec((None, k, LANE), lambda i: (layer, 0, col // LANE))
    return pl.pallas_call(
        _gates_kernel,
        grid=(t // tm,),
        in_specs=[pl.BlockSpec((tm, k), lambda i: (i, 0)), wspec(OFF_MGATE), wspec(OFF_DGATE), wspec(OFF_GLR)],
        out_specs=[pl.BlockSpec((tm, LANE), lambda i: (i, 0)), pl.BlockSpec((LANE, tm), lambda i: (0, i))],
        out_shape=[jax.ShapeDtypeStruct((t, LANE), F32), jax.ShapeDtypeStruct((LANE, t), F32)],
        compiler_params=_cparams("parallel"),
        name="gates",
    )(h, w_in, w_in, w_in)


def _cumsum_mats(n, blk):
    r = _iota2((n, n), 0)
    c = _iota2((n, n), 1)
    same = (r // blk) == (c // blk)
    lower = jnp.where(same & (c <= r), 1.0, 0.0).astype(F32)
    upper = jnp.where(same & (r <= c), 1.0, 0.0).astype(F32)
    return lower, upper


def _rms_head(x, w_row):
    return x * lax.rsqrt(jnp.mean(x * x, axis=-1, keepdims=True) + EPS) * w_row


def _mlstm_kernel(p_ref, g_ref, gt_ref, bcol_ref, brow_ref, nw_ref, o_ref, c_scr, m_scr, *, lb):
    nh, dk, dv = M_HEADS, M_DK, M_DV
    scale = dk ** -0.5

    @pl.when(pl.program_id(1) == 0)
    def _():
        c_scr[...] = jnp.zeros_like(c_scr)
        m_scr[...] = jnp.zeros_like(m_scr)

    gcol = g_ref[...] + bcol_ref[...]
    fcol = _log_sigmoid(gcol)
    grow = gt_ref[0:2 * SUBLANE, :] + brow_ref[0:2 * SUBLANE, :]
    frow = _log_sigmoid(grow)
    lower, upper = _cumsum_mats(CHUNK, CHUNK)
    r = _iota2((CHUNK, CHUNK), 0)
    c = _iota2((CHUNK, CHUNK), 1)
    causal = c <= r
    one_col = jnp.where(_iota2((CHUNK, LANE), 1) == 0, 1.0, 0.0).astype(F32)

    m_prev = [m_scr[h:h + 1, 0:1] for h in range(nh)]
    for ch in range(lb // CHUNK):
        r0 = ch * CHUNK
        bcum_col = _dot_f32(lower, fcol[r0:r0 + CHUNK, :])
        bcum_row = _dot_f32(frow[:, r0:r0 + CHUNK], upper)
        for h in range(nh):
            q = p_ref[r0:r0 + CHUNK, h * dk:(h + 1) * dk]
            k = p_ref[r0:r0 + CHUNK, nh * dk + h * dk:nh * dk + (h + 1) * dk]
            v = p_ref[r0:r0 + CHUNK, 2 * nh * dk + h * dv:2 * nh * dk + (h + 1) * dv]
            og = p_ref[r0:r0 + CHUNK, 2 * nh * dk + nh * dv + h * dv:2 * nh * dk + nh * dv + (h + 1) * dv]
            b_c = bcum_col[:, LN_MF + h:LN_MF + h + 1]
            i_c = gcol[r0:r0 + CHUNK, LN_MI + h:LN_MI + h + 1]
            b_r = bcum_row[LN_MF + h:LN_MF + h + 1, :]
            i_r = grow[LN_MI + h:LN_MI + h + 1, r0:r0 + CHUNK]
            mp = m_prev[h]

            log_d = jnp.where(causal, b_c - b_r + i_r, -jnp.inf)
            m_inter = b_c + mp
            m_t = jnp.maximum(m_inter, jnp.max(log_d, axis=-1, keepdims=True))
            s = _dot_nt(q, k) * (scale * jnp.exp(log_d - m_t))
            w_inter = jnp.exp(m_inter - m_t) * scale
            c_prev = c_scr[h]
            v_ext = jnp.concatenate([v.astype(F32), one_col], axis=1)
            tot = w_inter * _dot(q, c_prev.astype(BF16)) + _dot(s.astype(BF16), v_ext.astype(BF16))
            num = tot[:, :dv]
            den = tot[:, dv:dv + 1]
            hh = num / jnp.maximum(jnp.abs(den), jnp.exp(-m_t))
            y = _sigmoid(og.astype(F32)) * _rms_head(hh, nw_ref[h:h + 1, :])
            o_ref[r0:r0 + CHUNK, h * dv:(h + 1) * dv] = y.astype(o_ref.dtype)

            b_end = b_c[CHUNK - 1:CHUNK, :]
            log_w = b_end - b_c + i_c
            m_new = jnp.maximum(b_end + mp, jnp.max(log_w, axis=0, keepdims=True))
            w_col = jnp.exp(log_w - m_new)
            decay = jnp.exp(b_end + mp - m_new)
            c_scr[h] = decay * c_prev + _dot_tn(k, (w_col * v_ext).astype(BF16))
            m_prev[h] = m_new
    for h in range(nh):
        m_scr[h:h + 1, :] = jnp.broadcast_to(m_prev[h], (1, LANE))


def _mlstm(proj, gates, gates_t, gate_b, norm_w, bsz, seq):
    t = bsz * seq
    lb = _tile(seq, 256)
    nb = seq // lb
    bias = jnp.zeros((LANE,), F32).at[LN_MI:LN_MI + 2 * M_HEADS].set(gate_b)
    return pl.pallas_call(
        functools.partial(_mlstm_kernel, lb=lb),
        grid=(bsz, nb),
        in_specs=[
            pl.BlockSpec((lb, M_W), lambda b, c: (b * nb + c, 0)),
            pl.BlockSpec((lb, LANE), lambda b, c: (b * nb + c, 0)),
            pl.BlockSpec((LANE, lb), lambda b, c: (0, b * nb + c)),
            pl.BlockSpec((1, LANE), lambda b, c: (0, 0)),
            pl.BlockSpec((LANE, 1), lambda b, c: (0, 0)),
            pl.BlockSpec((M_HEADS, M_DV), lambda b, c: (0, 0)),
        ],
        out_specs=pl.BlockSpec((lb, BRANCH_W), lambda b, c: (b * nb + c, 0)),
        out_shape=jax.ShapeDtypeStruct((t, BRANCH_W), BF16),
        scratch_shapes=[pltpu.VMEM((M_HEADS, M_DK, M_DV + LANE), F32), pltpu.VMEM((SUBLANE, LANE), F32)],
        compiler_params=_cparams("parallel", "arbitrary"),
        name="mlstm",
    )(proj, gates, gates_t, bias.reshape(1, LANE), bias.reshape(LANE, 1), norm_w)


def _unit_lower_inverse(a):
    n = a.shape[0]
    r = _iota2((n, n), 0)
    c = _iota2((n, n), 1)
    eye = jnp.where(r == c, 1.0, 0.0).astype(F32)
    base = 8
    nd = jnp.where((r // base) == (c // base), -a, 0.0)
    n2 = _dot(nd.astype(BF16), nd.astype(BF16))
    x = eye + nd
    x = x + _dot(x.astype(BF16), n2.astype(BF16))
    n4 = _dot(n2.astype(BF16), n2.astype(BF16))
    x = x + _dot(x.astype(BF16), n4.astype(BF16))
    blk = base
    while blk < n:
        off = jnp.where(((r // (2 * blk)) == (c // (2 * blk))) & ((r // blk) != (c // blk)), a, 0.0)
        xa = _dot(x.astype(BF16), off.astype(BF16))
        x = x - _dot(xa.astype(BF16), x.astype(BF16))
        blk *= 2
    return x


def _gdn_kernel(p_ref, g_ref, gt_ref, pcol_ref, prow_ref, cw_ref, nw_ref, o_ref, s_scr, tail_scr, *, lb):
    nh, dk, dv = DN_HEADS, DN_DK, DN_DV
    scale = dk ** -0.5
    qkv_w = 2 * nh * dk + nh * dv

    @pl.when(pl.program_id(1) == 0)
    def _():
        s_scr[...] = jnp.zeros_like(s_scr)
        tail_scr[...] = jnp.zeros_like(tail_scr)

    gc = g_ref[...]
    beta_col = _sigmoid(gc)
    gl_col = -jnp.exp(pcol_ref[0:1, :]) * _softplus(gc + pcol_ref[1:2, :])
    gr = gt_ref[LN_DA - LN_DA % SUBLANE:LN_DA - LN_DA % SUBLANE + 2 * SUBLANE, :]
    pr = prow_ref[LN_DA - LN_DA % SUBLANE:LN_DA - LN_DA % SUBLANE + 2 * SUBLANE, :]
    gl_row = -jnp.exp(pr[:, 0:1]) * _softplus(gr + pr[:, 1:2])
    row0 = LN_DA % SUBLANE

    lower, upper = _cumsum_mats(CHUNK, CHUNK)
    r = _iota2((CHUNK, CHUNK), 0)
    c = _iota2((CHUNK, CHUNK), 1)
    incl = c <= r
    strict = c < r

    def conv_silu(col0, width):
        x = p_ref[:, col0:col0 + width].astype(F32)
        xp = jnp.concatenate([tail_scr[:, col0:col0 + width], x], axis=0)
        acc = cw_ref[DN_CONV - 1:DN_CONV, col0:col0 + width] * x
        for j in range(1, DN_CONV):
            acc = acc + cw_ref[DN_CONV - 1 - j:DN_CONV - j, col0:col0 + width] * pltpu.roll(xp, j, 0)[SUBLANE:, :]
        return _silu(acc)

    nch = lb // CHUNK
    gcum_cols = [_dot_f32(lower, gl_col[ch * CHUNK:(ch + 1) * CHUNK, :]) for ch in range(nch)]
    gcum_rows = [_dot_f32(gl_row[:, ch * CHUNK:(ch + 1) * CHUNK], upper) for ch in range(nch)]

    for h in range(nh):
        qa = conv_silu(h * dk, dk)
        ka = conv_silu(nh * dk + h * dk, dk)
        va = conv_silu(2 * nh * dk + h * dv, dv)
        qa = qa * lax.rsqrt(jnp.sum(qa * qa, axis=-1, keepdims=True) + EPS)
        ka = ka * lax.rsqrt(jnp.sum(ka * ka, axis=-1, keepdims=True) + EPS)
        s_state = s_scr[h]
        for ch in range(lb // CHUNK):
            r0 = ch * CHUNK
            q = qa[r0:r0 + CHUNK]
            k = ka[r0:r0 + CHUNK]
            v = va[r0:r0 + CHUNK]
            gcum_col = gcum_cols[ch][:, LN_DA + h:LN_DA + h + 1]
            gcum_row = gcum_rows[ch][row0 + h:row0 + h + 1, :]
            b_c = beta_col[r0:r0 + CHUNK, LN_DB + h:LN_DB + h + 1]
            diff = gcum_col - gcum_row
            decay = jnp.exp(jnp.where(incl, diff, -jnp.inf))
            kb = (k * b_c).astype(BF16)
            kk = k.astype(BF16)
            a = _dot_nt(kb, kk) * jnp.where(strict, decay, 0.0)
            p = _dot_nt(q.astype(BF16), kk) * (scale * decay)
            tinv = _unit_lower_inverse(a)
            eg = jnp.exp(gcum_col)
            rhs = jnp.concatenate([k * (b_c * eg), v * b_c], axis=1).astype(BF16)
            wu = _dot(tinv.astype(BF16), rhs)
            w = wu[:, :dk]
            u = wu[:, dk:]
            g_end = gcum_col[CHUNK - 1:CHUNK, :]
            qg = (q * (eg * scale)).astype(BF16)
            kd = (k * jnp.exp(g_end - gcum_col)).astype(BF16)
            sb = s_state.astype(BF16)
            u_new = u - _dot(w.astype(BF16), sb)
            o = _dot(qg, sb) + _dot(p.astype(BF16), u_new.astype(BF16))
            s_state = jnp.exp(g_end) * s_state + _dot_tn(kd, u_new.astype(BF16))
            z = p_ref[r0:r0 + CHUNK, qkv_w + h * dv:qkv_w + (h + 1) * dv].astype(F32)
            y = _rms_head(o, nw_ref[...]) * _silu(z)
            o_ref[r0:r0 + CHUNK, h * dv:(h + 1) * dv] = y.astype(o_ref.dtype)
        s_scr[h] = s_state
    tail_scr[...] = p_ref[lb - SUBLANE:lb, 0:qkv_w].astype(F32)


def _gdn(proj, gates, gates_t, conv_w, a_log, dt_bias, norm_w, bsz, seq):
    t = bsz * seq
    lb = _tile(seq, 256)
    nb = seq // lb
    qkv_w = 2 * DN_HEADS * DN_DK + DN_HEADS * DN_DV
    par = jnp.zeros((2, LANE), F32)
    par = par.at[0, LN_DA:LN_DA + DN_HEADS].set(a_log).at[1, LN_DA:LN_DA + DN_HEADS].set(dt_bias)
    return pl.pallas_call(
        functools.partial(_gdn_kernel, lb=lb),
        grid=(bsz, nb),
        in_specs=[
            pl.BlockSpec((lb, D_W), lambda b, c: (b * nb + c, 0)),
            pl.BlockSpec((lb, LANE), lambda b, c: (b * nb + c, 0)),
            pl.BlockSpec((LANE, lb), lambda b, c: (0, b * nb + c)),
            pl.BlockSpec((2, LANE), lambda b, c: (0, 0)),
            pl.BlockSpec((LANE, 2), lambda b, c: (0, 0)),
            pl.BlockSpec((DN_CONV, qkv_w), lambda b, c: (0, 0)),
            pl.BlockSpec((1, DN_DV), lambda b, c: (0, 0)),
        ],
        out_specs=pl.BlockSpec((lb, BRANCH_W), lambda b, c: (b * nb + c, 0)),
        out_shape=jax.ShapeDtypeStruct((t, BRANCH_W), BF16),
        scratch_shapes=[pltpu.VMEM((DN_HEADS, DN_DK, DN_DV), F32), pltpu.VMEM((SUBLANE, qkv_w), F32)],
        compiler_params=_cparams("parallel", "arbitrary"),
        name="gdn",
    )(proj, gates, gates_t, par, par.T, conv_w, norm_w.reshape(1, DN_DV))


def _gla_kernel(p_ref, g_ref, wa_ref, ba_ref, nw_ref, o_ref, s_scr, *, lb):
    nh, dk, dv = G_HEADS, G_DK, G_DV
    scale = dk ** -0.5
    grp = 64

    @pl.when(pl.program_id(1) == 0)
    def _():
        s_scr[...] = jnp.zeros_like(s_scr)

    log_a = _log_sigmoid(_dot(g_ref[...].astype(BF16), wa_ref[...].astype(BF16)) + ba_ref[...]) * (1.0 / G_TAU)
    lower, _ = _cumsum_mats(grp, G_CHUNK)
    r = _iota2((grp, grp), 0)
    c = _iota2((grp, grp), 1)
    same = (r // G_CHUNK) == (c // G_CHUNK)
    ones_blk = jnp.where(same, 1.0, 0.0).astype(F32)
    intra = same & (c <= r)

    for gi in range(lb // grp):
        r0 = gi * grp
        la = log_a[r0:r0 + grp, :]
        lc_all = _dot_f32(lower, la)
        le_all = _dot_f32(ones_blk, la)
        for h in range(nh):
            q = p_ref[r0:r0 + grp, h * dk:(h + 1) * dk].astype(F32)
            k = p_ref[r0:r0 + grp, nh * dk + h * dk:nh * dk + (h + 1) * dk].astype(F32)
            v = p_ref[r0:r0 + grp, 2 * nh * dk + h * dv:2 * nh * dk + (h + 1) * dv]
            rg = p_ref[r0:r0 + grp, 2 * nh * dk + nh * dv + h * dv:2 * nh * dk + nh * dv + (h + 1) * dv]
            lc = lc_all[:, h * dk:(h + 1) * dk]
            le = le_all[:, h * dk:(h + 1) * dk]
            q_in = (q * (jnp.exp(lc) * scale)).astype(BF16)
            k_in = (k * jnp.exp(-lc)).astype(BF16)
            kd = (k * jnp.exp(le - lc)).astype(BF16)
            dec = jnp.exp(le)
            p = jnp.where(intra, _dot_nt(q_in, k_in), 0.0)
            o_intra = _dot(p.astype(BF16), v)
            st = s_scr[h]
            outs = []
            for ci in range(grp // G_CHUNK):
                c0 = ci * G_CHUNK
                outs.append(_dot_nt(q_in[c0:c0 + G_CHUNK], st.astype(BF16)))
                st = dec[c0:c0 + 1, :] * st + _dot_tn(v[c0:c0 + G_CHUNK], kd[c0:c0 + G_CHUNK])
            s_scr[h] = st
            o = jnp.concatenate(outs, axis=0) + o_intra
            y = _rms_head(o, nw_ref[...]) * _silu(rg.astype(F32))
            o_ref[r0:r0 + grp, h * dv:(h + 1) * dv] = y.astype(o_ref.dtype)


def _gla(proj, gates, w_a2, b_a, norm_w, bsz, seq):
    t = bsz * seq
    lb = _tile(seq, 128)
    nb = seq // lb
    gw = G_HEADS * G_DK
    wa = jnp.zeros((LANE, gw), F32).at[LN_GLR:LN_GLR + G_RANK, :].set(w_a2)
    return pl.pallas_call(
        functools.partial(_gla_kernel, lb=lb),
        grid=(bsz, nb),
        in_specs=[
            pl.BlockSpec((lb, G_W), lambda b, c: (b * nb + c, 0)),
            pl.BlockSpec((lb, LANE), lambda b, c: (b * nb + c, 0)),
            pl.BlockSpec((LANE, gw), lambda b, c: (0, 0)),
            pl.BlockSpec((1, gw), lambda b, c: (0, 0)),
            pl.BlockSpec((1, G_DV), lambda b, c: (0, 0)),
        ],
        out_specs=pl.BlockSpec((lb, BRANCH_W), lambda b, c: (b * nb + c, 0)),
        out_shape=jax.ShapeDtypeStruct((t, BRANCH_W), BF16),
        scratch_shapes=[pltpu.VMEM((G_HEADS, G_DV, G_DK), F32)],
        compiler_params=_cparams("parallel", "arbitrary"),
        name="gla",
    )(proj, gates, wa, b_a.reshape(1, gw), norm_w.reshape(1, G_DV))


def _merge_kernel(ym_ref, yd_ref, yg_ref, wm_ref, wd_ref, wg_ref, gm_ref, gd_ref, gg_ref, o_ref):
    acc = _sigmoid(gm_ref[...].astype(F32)) * _dot(ym_ref[...], wm_ref[...].astype(BF16))
    acc = acc + _sigmoid(gd_ref[...].astype(F32)) * _dot(yd_ref[...], wd_ref[...].astype(BF16))
    acc = acc + _sigmoid(gg_ref[...].astype(F32)) * _dot(yg_ref[...], wg_ref[...].astype(BF16))
    o_ref[...] = acc.astype(o_ref.dtype)


def _merge(y_m, y_d, y_g, w_branch, layer, merge_pre, d_model):
    t = y_m.shape[0]
    tm = _tile(t, 1024)
    tn = _tile(d_model, 512)
    nj = d_model // tn
    yspec = pl.BlockSpec((tm, BRANCH_W), lambda i, j: (i, 0))
    wspec = lambda g: pl.BlockSpec((None, None, BRANCH_W, tn), lambda i, j: (layer, g, 0, j))
    gspec = lambda g: pl.BlockSpec((tm, tn), lambda i, j: (i, g * nj + j))
    return pl.pallas_call(
        _merge_kernel,
        grid=(t // tm, nj),
        in_specs=[yspec, yspec, yspec, wspec(0), wspec(1), wspec(2), gspec(0), gspec(1), gspec(2)],
        out_specs=pl.BlockSpec((tm, tn), lambda i, j: (i, j)),
        out_shape=jax.ShapeDtypeStruct((t, d_model), BF16),
        compiler_params=_cparams("parallel", "arbitrary"),
        name="merge",
    )(y_m, y_d, y_g, w_branch, w_branch, w_branch, merge_pre, merge_pre, merge_pre)


def _ffn_up_kernel(h_ref, wg_ref, wu_ref, cg_ref, cu_ref, o_ref, tail_scr, *, rows_per_seq):
    tm = h_ref.shape[0]

    @pl.when((pl.program_id(1) * tm) % rows_per_seq == 0)
    def _():
        tail_scr[...] = jnp.zeros_like(tail_scr)

    h = h_ref[...]

    def conv(w_ref, c_ref, slot):
        x = _dot(h, w_ref[...].astype(BF16))
        xp = jnp.concatenate([tail_scr[slot], x], axis=0)
        tail_scr[slot] = x[tm - SUBLANE:, :]
        acc = c_ref[FFN_CONV - 1:FFN_CONV, :] * x
        for j in range(1, FFN_CONV):
            acc = acc + c_ref[FFN_CONV - 1 - j:FFN_CONV - j, :] * pltpu.roll(xp, j, 0)[SUBLANE:, :]
        return acc

    gate = conv(wg_ref, cg_ref, 0)
    up = conv(wu_ref, cu_ref, 1)
    o_ref[...] = (_silu(gate) * up).astype(o_ref.dtype)


def _ffn_up(h, w_up, conv_w, layer, seq):
    t, k = h.shape
    d_ff = w_up.shape[-1] // 2
    tm = _tile(seq, 1024)
    tn = _tile(d_ff, 256)
    nj = d_ff // tn
    return pl.pallas_call(
        functools.partial(_ffn_up_kernel, rows_per_seq=seq),
        grid=(nj, t // tm),
        in_specs=[
            pl.BlockSpec((tm, k), lambda j, i: (i, 0)),
            pl.BlockSpec((None, k, tn), lambda j, i: (layer, 0, j)),
            pl.BlockSpec((None, k, tn), lambda j, i: (layer, 0, nj + j)),
            pl.BlockSpec((None, FFN_CONV, tn), lambda j, i: (layer, 0, j)),
            pl.BlockSpec((None, FFN_CONV, tn), lambda j, i: (layer, 0, nj + j)),
        ],
        out_specs=pl.BlockSpec((tm, tn), lambda j, i: (i, j)),
        out_shape=jax.ShapeDtypeStruct((t, d_ff), BF16),
        scratch_shapes=[pltpu.VMEM((2, SUBLANE, tn), F32)],
        compiler_params=_cparams("parallel", "arbitrary"),
        name="ffn_up",
    )(h, w_up, w_up, conv_w, conv_w)


def kernel(x, norm_mix_pre, norm_mix_post, norm_ffn_pre, norm_ffn_post, w_in, mlstm_gate_b, mlstm_norm, dn_conv, dn_a_log, dn_dt_bias, dn_norm, gla_w_a2, gla_b_a, gla_norm, w_branch, w_out, w_ffn_up, ffn_conv, w_ffn_down):
    bsz, seq, d_model = x.shape
    depth = w_in.shape[0]
    t = bsz * seq
    x2 = x.reshape(t, d_model)
    h = _norm_cast(x2, norm_mix_pre[0])
    for l in range(depth):
        proj_m = _matmul(h, w_in, l, OFF_M, M_W, BF16, name="proj_m")
        proj_d = _matmul(h, w_in, l, OFF_D, D_W, BF16, name="proj_d")
        proj_g = _matmul(h, w_in, l, OFF_G, G_W, BF16, name="proj_g")
        merge_pre = _matmul(h, w_in, l, OFF_MERGE, N_BRANCH * d_model, BF16, name="proj_merge")
        gates, gates_t = _gates(h, w_in, l)
        y_m = _mlstm(proj_m, gates, gates_t, mlstm_gate_b[l], mlstm_norm[l], bsz, seq)
        y_d = _gdn(proj_d, gates, gates_t, dn_conv[l], dn_a_log[l], dn_dt_bias[l], dn_norm[l], bsz, seq)
        y_g = _gla(proj_g, gates, gla_w_a2[l], gla_b_a[l], gla_norm[l], bsz, seq)
        merged = _merge(y_m, y_d, y_g, w_branch, l, merge_pre, d_model)
        mix = _matmul(merged, w_out, l, 0, d_model, F32, name="out_proj")
        x2, h = _resid_norm(x2, mix, norm_mix_post[l], norm_ffn_pre[l])
        act = _ffn_up(h, w_ffn_up, ffn_conv, l, seq)
        down = _matmul(act, w_ffn_down, l, 0, d_model, F32, tm=512, name="ffn_down")
        x2, h = _resid_norm(x2, down, norm_ffn_post[l], norm_mix_pre[l + 1] if l + 1 < depth else None)
    return x2.reshape(bsz, seq, d_model)
```

```python
import functools

import jax
import jax.numpy as jnp
from jax import lax
from jax.experimental import pallas as pl
from jax.experimental.pallas import tpu as pltpu

F32 = jnp.float32
BF16 = jnp.bfloat16
EPS = 1e-6
LANE = 128
SUBLANE = 8
MXU_WIDTH = 256
VMEM_LIMIT = 56 * 1024 * 1024

M_HEADS, M_DK, M_DV = 6, 128, 256
DN_HEADS, DN_DK, DN_DV, DN_CONV = 12, 128, 128, 4
G_HEADS, G_DK, G_DV, G_RANK, G_TAU = 6, 128, 256, 16, 16.0
CHUNK, G_CHUNK = 64, 16
N_BRANCH, BRANCH_W, FFN_CONV = 3, 1536, 3

M_W = 2 * M_HEADS * M_DK + 2 * M_HEADS * M_DV
DQKV_W = 2 * DN_HEADS * DN_DK + DN_HEADS * DN_DV
DZ_W = DN_HEADS * DN_DV
G_W = 2 * G_HEADS * G_DK + 2 * G_HEADS * G_DV
OFF_M = 0
OFF_MGATE = OFF_M + M_W
OFF_D = OFF_MGATE + 2 * M_HEADS
OFF_DZ = OFF_D + DQKV_W
OFF_DGATE = OFF_DZ + DZ_W
OFF_G = OFF_DGATE + 2 * DN_HEADS
OFF_GLR = OFF_G + G_W
OFF_MERGE = OFF_GLR + G_RANK
GATE_SRC = ((OFF_MGATE, 0, 2 * SUBLANE), (OFF_DGATE, 2 * SUBLANE, 2 * DN_HEADS), (OFF_GLR, 2 * SUBLANE + 2 * DN_HEADS, G_RANK))
LN_MI = 0
LN_MF = LN_MI + M_HEADS
LN_DB = GATE_SRC[1][1]
LN_DA = LN_DB + DN_HEADS
LN_GLR = GATE_SRC[2][1]
assert 2 * M_HEADS <= GATE_SRC[0][2] and LN_GLR + G_RANK <= LANE

ROW_TILE = 1024
COL_TILE_SLAB = 768
COL_TILE_WIDE = 1024
COL_TILE_FFN = 512
MIXER_ROWS = 256
GDN_ROWS = 128
GLA_ROWS = 128


def _cparams(*sem):
    return pltpu.CompilerParams(dimension_semantics=sem, vmem_limit_bytes=VMEM_LIMIT)


def _dot(a, b):
    return jnp.dot(a, b, preferred_element_type=F32)


def _dot_nt(a, b):
    return lax.dot_general(a, b, (((1,), (1,)), ((), ())), preferred_element_type=F32)


def _dot_tn(a, b):
    return lax.dot_general(a, b, (((0,), (0,)), ((), ())), preferred_element_type=F32)


def _dot_f32(a, b):
    return jnp.dot(a, b, preferred_element_type=F32, precision=lax.Precision.HIGHEST)


def _sigmoid(x):
    return 1.0 / (1.0 + jnp.exp(-x))


def _silu(x):
    return x * _sigmoid(x)


def _log_sigmoid(x):
    return jnp.minimum(x, 0.0) - jnp.log(1.0 + jnp.exp(-jnp.abs(x)))


def _softplus(x):
    return jnp.maximum(x, 0.0) + jnp.log(1.0 + jnp.exp(-jnp.abs(x)))


def _iota2(shape, axis):
    return lax.broadcasted_iota(jnp.int32, shape, axis)


def _tile(n, pref):
    t = min(n, pref)
    assert n % t == 0, (n, pref)
    return t


def _group_width(tn):
    return MXU_WIDTH if tn % MXU_WIDTH == 0 else tn


def _norm_cast_kernel(x_ref, g_ref, o_ref):
    x = x_ref[...]
    y = x * lax.rsqrt(jnp.mean(x * x, axis=-1, keepdims=True) + EPS)
    o_ref[...] = (y * g_ref[...]).astype(o_ref.dtype)


def _norm_cast(x2, g):
    t, d = x2.shape
    tr = _tile(t, 256)
    return pl.pallas_call(
        _norm_cast_kernel,
        grid=(t // tr,),
        in_specs=[pl.BlockSpec((tr, d), lambda i: (i, 0)), pl.BlockSpec((1, d), lambda i: (0, 0))],
        out_specs=pl.BlockSpec((tr, d), lambda i: (i, 0)),
        out_shape=jax.ShapeDtypeStruct((t, d), BF16),
        compiler_params=_cparams("parallel"),
        name="norm_cast",
    )(x2, g.reshape(1, d))


def _resid_norm_kernel(x_ref, y_ref, gp_ref, gn_ref, xo_ref, ho_ref):
    y = y_ref[...]
    yn = y * lax.rsqrt(jnp.mean(y * y, axis=-1, keepdims=True) + EPS) * gp_ref[...]
    xn = x_ref[...] + yn
    xo_ref[...] = xn
    hn = xn * lax.rsqrt(jnp.mean(xn * xn, axis=-1, keepdims=True) + EPS)
    ho_ref[...] = (hn * gn_ref[...]).astype(ho_ref.dtype)


def _resid_kernel(x_ref, y_ref, gp_ref, xo_ref):
    y = y_ref[...]
    yn = y * lax.rsqrt(jnp.mean(y * y, axis=-1, keepdims=True) + EPS) * gp_ref[...]
    xo_ref[...] = x_ref[...] + yn


def _resid_norm(x2, y, g_post, g_next):
    t, d = x2.shape
    tr = _tile(t, 256)
    row = pl.BlockSpec((tr, d), lambda i: (i, 0))
    vec = pl.BlockSpec((1, d), lambda i: (0, 0))
    if g_next is None:
        return pl.pallas_call(
            _resid_kernel,
            grid=(t // tr,),
            in_specs=[row, row, vec],
            out_specs=row,
            out_shape=jax.ShapeDtypeStruct((t, d), F32),
            compiler_params=_cparams("parallel"),
            name="resid",
        )(x2, y, g_post.reshape(1, d)), None
    return pl.pallas_call(
        _resid_norm_kernel,
        grid=(t // tr,),
        in_specs=[row, row, vec, vec],
        out_specs=[row, row],
        out_shape=[jax.ShapeDtypeStruct((t, d), F32), jax.ShapeDtypeStruct((t, d), BF16)],
        compiler_params=_cparams("parallel"),
        name="resid_norm",
    )(x2, y, g_post.reshape(1, d), g_next.reshape(1, d))


def _weight_window(w_hbm, layer, col, tn, k_minor):
    if k_minor:
        return w_hbm.at[pl.ds(col, tn), layer, :]
    return w_hbm.at[layer, :, pl.ds(pl.multiple_of(col, LANE), tn)]


def _load_weight_tile(w_hbm, stage, wb, sem, *, layer, col0, tn, nj, k_minor):
    j = pl.program_id(0)

    def copy(jj):
        return pltpu.make_async_copy(_weight_window(w_hbm, layer, col0 + jj * tn, tn, k_minor), stage, sem)

    @pl.when(j == 0)
    def _():
        copy(0).start()

    copy(j).wait()
    wb[...] = stage[...].astype(BF16)

    @pl.when(j + 1 < nj)
    def _():
        copy(j + 1).start()


def _tile_dot(h, wb, g0, gw, k_minor):
    if k_minor:
        return _dot_nt(h, wb[g0:g0 + gw, :])
    return _dot(h, wb[:, g0:g0 + gw])


def _wsmm_kernel(h_ref, w_hbm, o_ref, stage, wb, sem, *, tn, k_minor, **tile):
    @pl.when(pl.program_id(1) == 0)
    def _():
        _load_weight_tile(w_hbm, stage, wb, sem, tn=tn, k_minor=k_minor, **tile)

    h = h_ref[...]
    gw = _group_width(tn)
    for g0 in range(0, tn, gw):
        o_ref[:, g0:g0 + gw] = _tile_dot(h, wb, g0, gw, k_minor).astype(o_ref.dtype)


def _causal_conv(x, tail_ref, g0, w_ref, taps):
    tm, gw = x.shape
    xp = jnp.concatenate([tail_ref[:, g0:g0 + gw], x], axis=0)
    tail_ref[:, g0:g0 + gw] = x[tm - SUBLANE:, :]
    acc = w_ref[taps - 1:taps, g0:g0 + gw] * x
    for d in range(1, taps):
        acc = acc + w_ref[taps - 1 - d:taps - d, g0:g0 + gw] * pltpu.roll(xp, d, 0)[SUBLANE:, :]
    return acc


def _wsmm_conv_kernel(h_ref, w_hbm, cw_ref, o_ref, stage, wb, sem, tail, *, tn, k_minor, rows_per_seq, l2_tiles, **tile):
    j = pl.program_id(0)
    i = pl.program_id(1)
    tm = h_ref.shape[0]

    @pl.when(i == 0)
    def _():
        _load_weight_tile(w_hbm, stage, wb, sem, tn=tn, k_minor=k_minor, **tile)

    @pl.when((i * tm) % rows_per_seq == 0)
    def _():
        tail[...] = jnp.zeros_like(tail)

    h = h_ref[...]
    gw = _group_width(tn)
    for g0 in range(0, tn, gw):
        y = _silu(_causal_conv(_tile_dot(h, wb, g0, gw, k_minor), tail, g0, cw_ref, DN_CONV))
        for s0 in range(0, gw, DN_DK):
            ys = y[:, s0:s0 + DN_DK]
            yn = ys * lax.rsqrt(jnp.sum(ys * ys, axis=-1, keepdims=True) + EPS)
            o_ref[:, g0 + s0:g0 + s0 + DN_DK] = jnp.where(j < l2_tiles, yn, ys).astype(o_ref.dtype)


def _wsmm(h, w, layer, col0, n, out_dtype, *, tn, k_minor, name, conv=None, seq=None, l2_cols=0):
    t, k = h.shape
    tm = _tile(t, ROW_TILE if seq is None else min(ROW_TILE, seq))
    tn = _tile(n, tn)
    nj = n // tn
    tile = dict(layer=layer, col0=col0, nj=nj)
    wshape = (tn, k) if k_minor else (k, tn)
    in_specs = [pl.BlockSpec((tm, k), lambda j, i: (i, 0)), pl.BlockSpec(memory_space=pl.ANY)]
    scratch = [pltpu.VMEM(wshape, F32), pltpu.VMEM(wshape, BF16), pltpu.SemaphoreType.DMA(())]
    args = [h, w]
    if conv is None:
        body = functools.partial(_wsmm_kernel, tn=tn, k_minor=k_minor, **tile)
    else:
        assert seq % tm == 0 and l2_cols % tn == 0
        body = functools.partial(_wsmm_conv_kernel, tn=tn, k_minor=k_minor, rows_per_seq=seq, l2_tiles=l2_cols // tn, **tile)
        in_specs.append(pl.BlockSpec((conv.shape[0], tn), lambda j, i: (0, j)))
        scratch.append(pltpu.VMEM((SUBLANE, tn), F32))
        args.append(conv)
    return pl.pallas_call(
        body,
        grid=(nj, t // tm),
        in_specs=in_specs,
        out_specs=pl.BlockSpec((tm, tn), lambda j, i: (i, j)),
        out_shape=jax.ShapeDtypeStruct((t, n), out_dtype),
        scratch_shapes=scratch,
        compiler_params=_cparams("arbitrary", "arbitrary"),
        name=name,
    )(*args)


def _gates_kernel(h_ref, wt_hbm, g_ref, gt_ref, stage, wg, sem, *, layer):
    @pl.when(pl.program_id(0) == 0)
    def _():
        stage[...] = jnp.zeros_like(stage)
        copies = [pltpu.make_async_copy(wt_hbm.at[pl.ds(src, rows), layer, :], stage.at[pl.ds(dst, rows), :], sem.at[n])
                  for n, (src, dst, rows) in enumerate(GATE_SRC)]
        for cp in copies:
            cp.start()
        for cp in copies:
            cp.wait()
        wg[...] = stage[...].astype(BF16)

    gt = _dot_nt(wg[...], h_ref[...])
    gt_ref[...] = gt
    g_ref[...] = gt.T


def _gates(h, wt, layer):
    t, k = h.shape
    tm = _tile(t, 512)
    return pl.pallas_call(
        functools.partial(_gates_kernel, layer=layer),
        grid=(t // tm,),
        in_specs=[pl.BlockSpec((tm, k), lambda i: (i, 0)), pl.BlockSpec(memory_space=pl.ANY)],
        out_specs=[pl.BlockSpec((tm, LANE), lambda i: (i, 0)), pl.BlockSpec((LANE, tm), lambda i: (0, i))],
        out_shape=[jax.ShapeDtypeStruct((t, LANE), F32), jax.ShapeDtypeStruct((LANE, t), F32)],
        scratch_shapes=[pltpu.VMEM((LANE, k), F32), pltpu.VMEM((LANE, k), BF16), pltpu.SemaphoreType.DMA((len(GATE_SRC),))],
        compiler_params=_cparams("arbitrary"),
        name="gates",
    )(h, wt)


def _cumsum_mats(n, blk):
    r = _iota2((n, n), 0)
    c = _iota2((n, n), 1)
    same = (r // blk) == (c // blk)
    lower = jnp.where(same & (c <= r), 1.0, 0.0).astype(F32)
    upper = jnp.where(same & (r <= c), 1.0, 0.0).astype(F32)
    return lower, upper


def _rms_head(x, w_row):
    return x * lax.rsqrt(jnp.mean(x * x, axis=-1, keepdims=True) + EPS) * w_row


def _mlstm_kernel(p_ref, g_ref, gt_ref, bcol_ref, brow_ref, nw_ref, o_ref, c_scr, m_scr, *, lb):
    nh, dk, dv = M_HEADS, M_DK, M_DV
    scale = dk ** -0.5

    @pl.when(pl.program_id(1) == 0)
    def _():
        c_scr[...] = jnp.zeros_like(c_scr)
        m_scr[...] = jnp.zeros_like(m_scr)

    gcol = g_ref[...] + bcol_ref[...]
    fcol = _log_sigmoid(gcol)
    grow = gt_ref[0:2 * SUBLANE, :] + brow_ref[0:2 * SUBLANE, :]
    frow = _log_sigmoid(grow)
    lower, upper = _cumsum_mats(CHUNK, CHUNK)
    r = _iota2((CHUNK, CHUNK), 0)
    c = _iota2((CHUNK, CHUNK), 1)
    causal = c <= r
    one_col = jnp.where(_iota2((CHUNK, LANE), 1) == 0, 1.0, 0.0).astype(F32)

    m_prev = [m_scr[h:h + 1, 0:1] for h in range(nh)]
    for ch in range(lb // CHUNK):
        r0 = ch * CHUNK
        bcum_col = _dot_f32(lower, fcol[r0:r0 + CHUNK, :])
        bcum_row = _dot_f32(frow[:, r0:r0 + CHUNK], upper)
        for h in range(nh):
            q = p_ref[r0:r0 + CHUNK, h * dk:(h + 1) * dk]
            k = p_ref[r0:r0 + CHUNK, nh * dk + h * dk:nh * dk + (h + 1) * dk]
            v = p_ref[r0:r0 + CHUNK, 2 * nh * dk + h * dv:2 * nh * dk + (h + 1) * dv]
            og = p_ref[r0:r0 + CHUNK, 2 * nh * dk + nh * dv + h * dv:2 * nh * dk + nh * dv + (h + 1) * dv]
            b_c = bcum_col[:, LN_MF + h:LN_MF + h + 1]
            i_c = gcol[r0:r0 + CHUNK, LN_MI + h:LN_MI + h + 1]
            b_r = bcum_row[LN_MF + h:LN_MF + h + 1, :]
            i_r = grow[LN_MI + h:LN_MI + h + 1, r0:r0 + CHUNK]
            mp = m_prev[h]

            log_d = jnp.where(causal, b_c - b_r + i_r, -jnp.inf)
            m_inter = b_c + mp
            m_t = jnp.maximum(m_inter, jnp.max(log_d, axis=-1, keepdims=True))
            s = _dot_nt(q, k) * (scale * jnp.exp(log_d - m_t))
            w_inter = jnp.exp(m_inter - m_t) * scale
            c_prev = c_scr[h]
            v_ext = jnp.concatenate([v.astype(F32), one_col], axis=1)
            tot = w_inter * _dot(q, c_prev.astype(BF16)) + _dot(s.astype(BF16), v_ext.astype(BF16))
            num = tot[:, :dv]
            den = tot[:, dv:dv + 1]
            hh = num / jnp.maximum(jnp.abs(den), jnp.exp(-m_t))
            y = _sigmoid(og.astype(F32)) * _rms_head(hh, nw_ref[h:h + 1, :])
            o_ref[r0:r0 + CHUNK, h * dv:(h + 1) * dv] = y.astype(o_ref.dtype)

            b_end = b_c[CHUNK - 1:CHUNK, :]
            log_w = b_end - b_c + i_c
            m_new = jnp.maximum(b_end + mp, jnp.max(log_w, axis=0, keepdims=True))
            w_col = jnp.exp(log_w - m_new)
            decay = jnp.exp(b_end + mp - m_new)
            c_scr[h] = decay * c_prev + _dot_tn(k, (w_col * v_ext).astype(BF16))
            m_prev[h] = m_new
    for h in range(nh):
        m_scr[h:h + 1, :] = jnp.broadcast_to(m_prev[h], (1, LANE))


def _mlstm(proj, gates, gates_t, gate_b, norm_w, bsz, seq):
    t = bsz * seq
    lb = _tile(seq, MIXER_ROWS)
    nb = seq // lb
    bias = jnp.zeros((LANE,), F32).at[LN_MI:LN_MI + 2 * M_HEADS].set(gate_b)
    return pl.pallas_call(
        functools.partial(_mlstm_kernel, lb=lb),
        grid=(bsz, nb),
        in_specs=[
            pl.BlockSpec((lb, M_W), lambda b, c: (b * nb + c, 0)),
            pl.BlockSpec((lb, LANE), lambda b, c: (b * nb + c, 0)),
            pl.BlockSpec((LANE, lb), lambda b, c: (0, b * nb + c)),
            pl.BlockSpec((1, LANE), lambda b, c: (0, 0)),
            pl.BlockSpec((LANE, 1), lambda b, c: (0, 0)),
            pl.BlockSpec((M_HEADS, M_DV), lambda b, c: (0, 0)),
        ],
        out_specs=pl.BlockSpec((lb, BRANCH_W), lambda b, c: (b * nb + c, 0)),
        out_shape=jax.ShapeDtypeStruct((t, BRANCH_W), BF16),
        scratch_shapes=[pltpu.VMEM((M_HEADS, M_DK, M_DV + LANE), F32), pltpu.VMEM((SUBLANE, LANE), F32)],
        compiler_params=_cparams("parallel", "arbitrary"),
        name="mlstm",
    )(proj, gates, gates_t, bias.reshape(1, LANE), bias.reshape(LANE, 1), norm_w)


def _unit_lower_inverses(a_list, blk_diag):
    n = a_list[0].shape[0]
    r = _iota2((n, n), 0)
    c = _iota2((n, n), 1)
    eye = jnp.where(r == c, 1.0, 0.0).astype(F32)
    base = SUBLANE
    in_base = (r // base) == (c // base)
    nd = [jnp.where(in_base, -a, 0.0).astype(BF16) for a in a_list]
    n2 = [_dot(m, m).astype(BF16) for m in nd]
    xs = [eye + m.astype(F32) for m in nd]
    xs = [x + _dot(x.astype(BF16), m) for x, m in zip(xs, n2)]
    n4 = [_dot(m, m).astype(BF16) for m in n2]
    xs = [x + _dot(x.astype(BF16), m) for x, m in zip(xs, n4)]
    blk = base
    while blk < blk_diag:
        sel = ((r // (2 * blk)) == (c // (2 * blk))) & ((r // blk) != (c // blk))
        offs = [jnp.where(sel, a, 0.0).astype(BF16) for a in a_list]
        xb = [x.astype(BF16) for x in xs]
        xa = [_dot(x, o).astype(BF16) for x, o in zip(xb, offs)]
        xs = [x - _dot(m, x_b) for x, m, x_b in zip(xs, xa, xb)]
        blk *= 2
    return xs


def _gdn_kernel(p_ref, z_ref, g_ref, gt_ref, pcol_ref, prow_ref, nw_ref, o_ref,
                s_scr, w_scr, u_scr, qg_scr, kd_scr, pm_scr, qs_scr, *, lb):
    nh, dk, dv = DN_HEADS, DN_DK, DN_DV
    scale = dk ** -0.5

    @pl.when(pl.program_id(1) == 0)
    def _():
        s_scr[...] = jnp.zeros_like(s_scr)

    gc = g_ref[...]
    beta_col = _sigmoid(gc)
    gl_col = -jnp.exp(pcol_ref[0:1, :]) * _softplus(gc + pcol_ref[1:2, :])
    row0 = LN_DA % SUBLANE
    rbase = LN_DA - row0
    gr = gt_ref[rbase:rbase + 2 * SUBLANE, :]
    pr = prow_ref[rbase:rbase + 2 * SUBLANE, :]
    gl_row = -jnp.exp(pr[:, 0:1]) * _softplus(gr + pr[:, 1:2])

    lower, upper = _cumsum_mats(lb, CHUNK)
    r = _iota2((lb, lb), 0)
    c = _iota2((lb, lb), 1)
    same = (r // CHUNK) == (c // CHUNK)
    incl = same & (c <= r)
    strict = same & (c < r)
    ones_bd = jnp.where(same, 1.0, 0.0).astype(F32)
    gcum_col = _dot_f32(lower, gl_col)
    gtot_col = _dot_f32(ones_bd, gl_col)
    gcum_row = _dot_f32(gl_row, upper)

    ks, qs, vs, a_list, gccs, bcs = [], [], [], [], [], []
    for h in range(nh):
        qa = p_ref[:, h * dk:(h + 1) * dk]
        ka = p_ref[:, nh * dk + h * dk:nh * dk + (h + 1) * dk]
        va = p_ref[:, 2 * nh * dk + h * dv:2 * nh * dk + (h + 1) * dv]
        gcc = gcum_col[:, LN_DA + h:LN_DA + h + 1]
        gcr = gcum_row[row0 + h:row0 + h + 1, :]
        b_c = beta_col[:, LN_DB + h:LN_DB + h + 1]
        decay = jnp.exp(jnp.where(incl, gcc - gcr, -jnp.inf))
        a_list.append(_dot_nt((ka.astype(F32) * b_c).astype(BF16), ka) * jnp.where(strict, decay, 0.0))
        pm_scr[h] = (_dot_nt(qa, ka) * (scale * decay)).astype(BF16)
        ks.append(ka.astype(F32)); qs.append(qa.astype(F32)); vs.append(va.astype(F32)); gccs.append(gcc); bcs.append(b_c)
    xs = _unit_lower_inverses(a_list, CHUNK)
    for h in range(nh):
        eg = jnp.exp(gccs[h])
        rhs = jnp.concatenate([ks[h] * (bcs[h] * eg), vs[h] * bcs[h]], axis=1).astype(BF16)
        wu = _dot(xs[h].astype(BF16), rhs)
        w_scr[h] = wu[:, :dk].astype(BF16)
        u_scr[h] = wu[:, dk:]
        qg_scr[h] = (qs[h] * (eg * scale)).astype(BF16)
        gtot = gtot_col[:, LN_DA + h:LN_DA + h + 1]
        kd_scr[h] = (ks[h] * jnp.exp(gtot - gccs[h])).astype(BF16)

    for ch in range(lb // CHUNK):
        r0 = ch * CHUNK
        dec_row = jnp.exp(gtot_col[r0:r0 + 1, :])
        for h in range(nh):
            s_prev = s_scr[h]
            sb = s_prev.astype(BF16)
            u_new = u_scr[h, r0:r0 + CHUNK, :] - _dot(w_scr[h, r0:r0 + CHUNK, :], sb)
            qs_scr[h, r0:r0 + CHUNK, :] = _dot(qg_scr[h, r0:r0 + CHUNK, :], sb)
            u_scr[h, r0:r0 + CHUNK, :] = u_new
            s_scr[h] = dec_row[:, LN_DA + h:LN_DA + h + 1] * s_prev + _dot_tn(kd_scr[h, r0:r0 + CHUNK, :], u_new.astype(BF16))

    for h in range(nh):
        o = qs_scr[h] + _dot(pm_scr[h], u_scr[h].astype(BF16))
        z = z_ref[:, h * dv:(h + 1) * dv].astype(F32)
        o_ref[:, h * dv:(h + 1) * dv] = (_rms_head(o, nw_ref[...]) * _silu(z)).astype(o_ref.dtype)


def _gdn(qkv, z, gates, gates_t, a_log, dt_bias, norm_w, bsz, seq):
    t = bsz * seq
    lb = _tile(seq, GDN_ROWS)
    nb = seq // lb
    nh = DN_HEADS
    par = jnp.zeros((2, LANE), F32)
    par = par.at[0, LN_DA:LN_DA + nh].set(a_log).at[1, LN_DA:LN_DA + nh].set(dt_bias)
    return pl.pallas_call(
        functools.partial(_gdn_kernel, lb=lb),
        grid=(bsz, nb),
        in_specs=[
            pl.BlockSpec((lb, DQKV_W), lambda b, c: (b * nb + c, 0)),
            pl.BlockSpec((lb, DZ_W), lambda b, c: (b * nb + c, 0)),
            pl.BlockSpec((lb, LANE), lambda b, c: (b * nb + c, 0)),
            pl.BlockSpec((LANE, lb), lambda b, c: (0, b * nb + c)),
            pl.BlockSpec((2, LANE), lambda b, c: (0, 0)),
            pl.BlockSpec((LANE, 2), lambda b, c: (0, 0)),
            pl.BlockSpec((1, DN_DV), lambda b, c: (0, 0)),
        ],
        out_specs=pl.BlockSpec((lb, BRANCH_W), lambda b, c: (b * nb + c, 0)),
        out_shape=jax.ShapeDtypeStruct((t, BRANCH_W), BF16),
        scratch_shapes=[
            pltpu.VMEM((nh, DN_DK, DN_DV), F32),
            pltpu.VMEM((nh, lb, DN_DK), BF16),
            pltpu.VMEM((nh, lb, DN_DV), F32),
            pltpu.VMEM((nh, lb, DN_DK), BF16),
            pltpu.VMEM((nh, lb, DN_DK), BF16),
            pltpu.VMEM((nh, lb, lb), BF16),
            pltpu.VMEM((nh, lb, DN_DV), F32),
        ],
        compiler_params=_cparams("parallel", "arbitrary"),
        name="gdn",
    )(qkv, z, gates, gates_t, par, par.T, norm_w.reshape(1, DN_DV))


def _gla_kernel(p_ref, g_ref, wa_ref, ba_ref, nw_ref, o_ref, s_scr, *, lb):
    nh, dk, dv = G_HEADS, G_DK, G_DV
    scale = dk ** -0.5
    grp = 64

    @pl.when(pl.program_id(1) == 0)
    def _():
        s_scr[...] = jnp.zeros_like(s_scr)

    log_a = _log_sigmoid(_dot(g_ref[...].astype(BF16), wa_ref[...].astype(BF16)) + ba_ref[...]) * (1.0 / G_TAU)
    lower, _ = _cumsum_mats(grp, G_CHUNK)
    r = _iota2((grp, grp), 0)
    c = _iota2((grp, grp), 1)
    same = (r // G_CHUNK) == (c // G_CHUNK)
    ones_blk = jnp.where(same, 1.0, 0.0).astype(F32)
    intra = same & (c <= r)

    for gi in range(lb // grp):
        r0 = gi * grp
        la = log_a[r0:r0 + grp, :]
        lc_all = _dot_f32(lower, la)
        le_all = _dot_f32(ones_blk, la)
        q_ins, kds, decs, vs, o_intras = [], [], [], [], []
        for h in range(nh):
            q = p_ref[r0:r0 + grp, h * dk:(h + 1) * dk].astype(F32)
            k = p_ref[r0:r0 + grp, nh * dk + h * dk:nh * dk + (h + 1) * dk].astype(F32)
            v = p_ref[r0:r0 + grp, 2 * nh * dk + h * dv:2 * nh * dk + (h + 1) * dv]
            lc = lc_all[:, h * dk:(h + 1) * dk]
            le = le_all[:, h * dk:(h + 1) * dk]
            q_in = (q * (jnp.exp(lc) * scale)).astype(BF16)
            k_in = (k * jnp.exp(-lc)).astype(BF16)
            p = jnp.where(intra, _dot_nt(q_in, k_in), 0.0)
            q_ins.append(q_in)
            kds.append((k * jnp.exp(le - lc)).astype(BF16))
            decs.append(jnp.exp(le))
            vs.append(v)
            o_intras.append(_dot(p.astype(BF16), v))
        outs = [[] for _ in range(nh)]
        for ci in range(grp // G_CHUNK):
            c0 = ci * G_CHUNK
            for h in range(nh):
                st = s_scr[h]
                outs[h].append(_dot_nt(q_ins[h][c0:c0 + G_CHUNK], st.astype(BF16)))
                s_scr[h] = decs[h][c0:c0 + 1, :] * st + _dot_tn(vs[h][c0:c0 + G_CHUNK], kds[h][c0:c0 + G_CHUNK])
        for h in range(nh):
            rg = p_ref[r0:r0 + grp, 2 * nh * dk + nh * dv + h * dv:2 * nh * dk + nh * dv + (h + 1) * dv]
            o = jnp.concatenate(outs[h], axis=0) + o_intras[h]
            y = _rms_head(o, nw_ref[...]) * _silu(rg.astype(F32))
            o_ref[r0:r0 + grp, h * dv:(h + 1) * dv] = y.astype(o_ref.dtype)


def _gla(proj, gates, w_a2, b_a, norm_w, bsz, seq):
    t = bsz * seq
    lb = _tile(seq, GLA_ROWS)
    nb = seq // lb
    gw = G_HEADS * G_DK
    wa = jnp.zeros((LANE, gw), F32).at[LN_GLR:LN_GLR + G_RANK, :].set(w_a2)
    return pl.pallas_call(
        functools.partial(_gla_kernel, lb=lb),
        grid=(bsz, nb),
        in_specs=[
            pl.BlockSpec((lb, G_W), lambda b, c: (b * nb + c, 0)),
            pl.BlockSpec((lb, LANE), lambda b, c: (b * nb + c, 0)),
            pl.BlockSpec((LANE, gw), lambda b, c: (0, 0)),
            pl.BlockSpec((1, gw), lambda b, c: (0, 0)),
            pl.BlockSpec((1, G_DV), lambda b, c: (0, 0)),
        ],
        out_specs=pl.BlockSpec((lb, BRANCH_W), lambda b, c: (b * nb + c, 0)),
        out_shape=jax.ShapeDtypeStruct((t, BRANCH_W), BF16),
        scratch_shapes=[pltpu.VMEM((G_HEADS, G_DV, G_DK), F32)],
        compiler_params=_cparams("parallel", "arbitrary"),
        name="gla",
    )(proj, gates, wa, b_a.reshape(1, gw), norm_w.reshape(1, G_DV))


def _merge_kernel(ym_ref, yd_ref, yg_ref, wm_ref, wd_ref, wg_ref, gm_ref, gd_ref, gg_ref, o_ref):
    acc = _sigmoid(gm_ref[...].astype(F32)) * _dot(ym_ref[...], wm_ref[...].astype(BF16))
    acc = acc + _sigmoid(gd_ref[...].astype(F32)) * _dot(yd_ref[...], wd_ref[...].astype(BF16))
    acc = acc + _sigmoid(gg_ref[...].astype(F32)) * _dot(yg_ref[...], wg_ref[...].astype(BF16))
    o_ref[...] = acc.astype(o_ref.dtype)


def _merge(y_m, y_d, y_g, w_branch, layer, merge_pre, d_model):
    t = y_m.shape[0]
    tm = _tile(t, ROW_TILE)
    tn = _tile(d_model, COL_TILE_FFN)
    nj = d_model // tn
    yspec = pl.BlockSpec((tm, BRANCH_W), lambda i, j: (i, 0))
    wspec = lambda g: pl.BlockSpec((None, None, BRANCH_W, tn), lambda i, j: (layer, g, 0, j))
    gspec = lambda g: pl.BlockSpec((tm, tn), lambda i, j: (i, g * nj + j))
    return pl.pallas_call(
        _merge_kernel,
        grid=(t // tm, nj),
        in_specs=[yspec, yspec, yspec, wspec(0), wspec(1), wspec(2), gspec(0), gspec(1), gspec(2)],
        out_specs=pl.BlockSpec((tm, tn), lambda i, j: (i, j)),
        out_shape=jax.ShapeDtypeStruct((t, d_model), BF16),
        compiler_params=_cparams("parallel", "arbitrary"),
        name="merge",
    )(y_m, y_d, y_g, w_branch, w_branch, w_branch, merge_pre, merge_pre, merge_pre)


def _ffn_up_kernel(h_ref, w_hbm, cg_ref, cu_ref, o_ref, stage, wb, sem, tail_g, tail_u, *, tn, d_ff, rows_per_seq, layer, nj):
    i = pl.program_id(1)
    tm = h_ref.shape[0]

    @pl.when(i == 0)
    def _():
        for m, col0 in enumerate((0, d_ff)):
            _load_weight_tile(w_hbm, stage.at[m], wb.at[m], sem.at[m], layer=layer, col0=col0, tn=tn, nj=nj, k_minor=False)

    @pl.when((i * tm) % rows_per_seq == 0)
    def _():
        tail_g[...] = jnp.zeros_like(tail_g)
        tail_u[...] = jnp.zeros_like(tail_u)

    h = h_ref[...]
    gw = _group_width(tn)
    for g0 in range(0, tn, gw):
        gate = _causal_conv(_dot(h, wb[0, :, g0:g0 + gw]), tail_g, g0, cg_ref, FFN_CONV)
        up = _causal_conv(_dot(h, wb[1, :, g0:g0 + gw]), tail_u, g0, cu_ref, FFN_CONV)
        o_ref[:, g0:g0 + gw] = (_silu(gate) * up).astype(o_ref.dtype)


def _ffn_up(h, w_up, conv_w, layer, seq):
    t, k = h.shape
    d_ff = w_up.shape[-1] // 2
    tm = _tile(seq, ROW_TILE)
    tn = _tile(d_ff, COL_TILE_FFN)
    nj = d_ff // tn
    return pl.pallas_call(
        functools.partial(_ffn_up_kernel, tn=tn, d_ff=d_ff, rows_per_seq=seq, layer=layer, nj=nj),
        grid=(nj, t // tm),
        in_specs=[
            pl.BlockSpec((tm, k), lambda j, i: (i, 0)),
            pl.BlockSpec(memory_space=pl.ANY),
            pl.BlockSpec((None, FFN_CONV, tn), lambda j, i: (layer, 0, j)),
            pl.BlockSpec((None, FFN_CONV, tn), lambda j, i: (layer, 0, nj + j)),
        ],
        out_specs=pl.BlockSpec((tm, tn), lambda j, i: (i, j)),
        out_shape=jax.ShapeDtypeStruct((t, d_ff), BF16),
        scratch_shapes=[pltpu.VMEM((2, k, tn), F32), pltpu.VMEM((2, k, tn), BF16), pltpu.SemaphoreType.DMA((2,)),
                        pltpu.VMEM((SUBLANE, tn), F32), pltpu.VMEM((SUBLANE, tn), F32)],
        compiler_params=_cparams("arbitrary", "arbitrary"),
        name="ffn_up",
    )(h, w_up, conv_w, conv_w)


def kernel(x, norm_mix_pre, norm_mix_post, norm_ffn_pre, norm_ffn_post, w_in, mlstm_gate_b, mlstm_norm, dn_conv, dn_a_log, dn_dt_bias, dn_norm, gla_w_a2, gla_b_a, gla_norm, w_branch, w_out, w_ffn_up, ffn_conv, w_ffn_down):
    bsz, seq, d_model = x.shape
    depth = w_in.shape[0]
    t = bsz * seq
    x2 = x.reshape(t, d_model)
    wt = jnp.transpose(w_in, (2, 0, 1))
    proj = functools.partial(_wsmm, k_minor=True)
    dense = functools.partial(_wsmm, k_minor=False)
    h = _norm_cast(x2, norm_mix_pre[0])
    for l in range(depth):
        proj_m = proj(h, wt, l, OFF_M, M_W, BF16, tn=COL_TILE_SLAB, name="proj_m")
        d_qkv = proj(h, wt, l, OFF_D, DQKV_W, BF16, tn=COL_TILE_SLAB, name="proj_dqkv",
                     conv=dn_conv[l], seq=seq, l2_cols=2 * DN_HEADS * DN_DK)
        d_z = proj(h, wt, l, OFF_DZ, DZ_W, BF16, tn=COL_TILE_SLAB, name="proj_dz")
        proj_g = proj(h, wt, l, OFF_G, G_W, BF16, tn=COL_TILE_SLAB, name="proj_g")
        merge_pre = proj(h, wt, l, OFF_MERGE, N_BRANCH * d_model, BF16, tn=COL_TILE_WIDE, name="proj_merge")
        gates, gates_t = _gates(h, wt, l)
        y_m = _mlstm(proj_m, gates, gates_t, mlstm_gate_b[l], mlstm_norm[l], bsz, seq)
        y_d = _gdn(d_qkv, d_z, gates, gates_t, dn_a_log[l], dn_dt_bias[l], dn_norm[l], bsz, seq)
        y_g = _gla(proj_g, gates, gla_w_a2[l], gla_b_a[l], gla_norm[l], bsz, seq)
        merged = _merge(y_m, y_d, y_g, w_branch, l, merge_pre, d_model)
        mix = dense(merged, w_out, l, 0, d_model, F32, tn=COL_TILE_WIDE, name="out_proj")
        x2, h = _resid_norm(x2, mix, norm_mix_post[l], norm_ffn_pre[l])
        act = _ffn_up(h, w_ffn_up, ffn_conv, l, seq)
        down = dense(act, w_ffn_down, l, 0, d_model, F32, tn=COL_TILE_FFN, name="ffn_down")
        x2, h = _resid_norm(x2, down, norm_ffn_post[l], norm_mix_pre[l + 1] if l + 1 < depth else None)
    return x2.reshape(bsz, seq, d_model)
```

```python
import functools

import jax
import jax.numpy as jnp
from jax import lax
from jax.experimental import pallas as pl
from jax.experimental.pallas import tpu as pltpu

F32 = jnp.float32
BF16 = jnp.bfloat16
EPS = 1e-6
LANE = 128
SUBLANE = 8
MXU_WIDTH = 256
VMEM_LIMIT = 56 * 1024 * 1024

M_HEADS, M_DK, M_DV = 6, 128, 256
DN_HEADS, DN_DK, DN_DV, DN_CONV = 12, 128, 128, 4
G_HEADS, G_DK, G_DV, G_RANK, G_TAU = 6, 128, 256, 16, 16.0
CHUNK, G_CHUNK = 64, 16
N_BRANCH, BRANCH_W, FFN_CONV = 3, 1536, 3

M_W = 2 * M_HEADS * M_DK + 2 * M_HEADS * M_DV
DQKV_W = 2 * DN_HEADS * DN_DK + DN_HEADS * DN_DV
DZ_W = DN_HEADS * DN_DV
G_W = 2 * G_HEADS * G_DK + 2 * G_HEADS * G_DV
OFF_M = 0
OFF_MGATE = OFF_M + M_W
OFF_D = OFF_MGATE + 2 * M_HEADS
OFF_DZ = OFF_D + DQKV_W
OFF_DGATE = OFF_DZ + DZ_W
OFF_G = OFF_DGATE + 2 * DN_HEADS
OFF_GLR = OFF_G + G_W
OFF_MERGE = OFF_GLR + G_RANK
GATE_SRC = ((OFF_MGATE, 0, 2 * SUBLANE), (OFF_DGATE, 2 * SUBLANE, 2 * DN_HEADS), (OFF_GLR, 2 * SUBLANE + 2 * DN_HEADS, G_RANK))
LN_MI = 0
LN_MF = LN_MI + M_HEADS
LN_DB = GATE_SRC[1][1]
LN_DA = LN_DB + DN_HEADS
LN_GLR = GATE_SRC[2][1]
assert 2 * M_HEADS <= GATE_SRC[0][2] and LN_GLR + G_RANK <= LANE

ROW_TILE = 1024
COL_TILE_SLAB = 768
COL_TILE_WIDE = 1024
COL_TILE_FFN = 512
EPILOGUE_ROW_SPLIT = 4
FFN_ROW_SPLIT = 8
MERGE_ROW_SPLIT = 4
MIXER_ROWS = 256
GDN_ROWS = 128
GLA_ROWS = 128


def _cparams(*sem):
    return pltpu.CompilerParams(dimension_semantics=sem, vmem_limit_bytes=VMEM_LIMIT)


def _dot(a, b):
    return jnp.dot(a, b, preferred_element_type=F32)


def _dot_nt(a, b):
    return lax.dot_general(a, b, (((1,), (1,)), ((), ())), preferred_element_type=F32)


def _dot_tn(a, b):
    return lax.dot_general(a, b, (((0,), (0,)), ((), ())), preferred_element_type=F32)


def _split3(x):
    hi = x.astype(BF16)
    r1 = x - hi.astype(F32)
    mid = r1.astype(BF16)
    lo = (r1 - mid.astype(F32)).astype(BF16)
    return hi, mid, lo


def _sum01(m01, x):
    m = m01.astype(BF16)
    hi, mid, lo = _split3(x)
    return _dot(m, hi) + _dot(m, mid) + _dot(m, lo)


def _sum01_t(x, m01):
    m = m01.astype(BF16)
    hi, mid, lo = _split3(x)
    return _dot(hi, m) + _dot(mid, m) + _dot(lo, m)


def _sigmoid(x):
    return 1.0 / (1.0 + jnp.exp(-x))


def _silu(x):
    return x * _sigmoid(x)


def _log_sigmoid(x):
    return jnp.minimum(x, 0.0) - jnp.log(1.0 + jnp.exp(-jnp.abs(x)))


def _softplus(x):
    return jnp.maximum(x, 0.0) + jnp.log(1.0 + jnp.exp(-jnp.abs(x)))


def _iota2(shape, axis):
    return lax.broadcasted_iota(jnp.int32, shape, axis)


def _tile(n, pref):
    t = min(n, pref)
    assert n % t == 0, (n, pref)
    return t


def _group_width(tn):
    return MXU_WIDTH if tn % MXU_WIDTH == 0 else tn


def _norm_cast_kernel(x_ref, g_ref, o_ref):
    x = x_ref[...]
    y = x * lax.rsqrt(jnp.mean(x * x, axis=-1, keepdims=True) + EPS)
    o_ref[...] = (y * g_ref[...]).astype(o_ref.dtype)


def _norm_cast(x2, g):
    t, d = x2.shape
    tr = _tile(t, 256)
    return pl.pallas_call(
        _norm_cast_kernel,
        grid=(t // tr,),
        in_specs=[pl.BlockSpec((tr, d), lambda i: (i, 0)), pl.BlockSpec((1, d), lambda i: (0, 0))],
        out_specs=pl.BlockSpec((tr, d), lambda i: (i, 0)),
        out_shape=jax.ShapeDtypeStruct((t, d), BF16),
        compiler_params=_cparams("parallel"),
        name="norm_cast",
    )(x2, g.reshape(1, d))


def _resid_norm_kernel(x_ref, y_ref, gp_ref, gn_ref, xo_ref, ho_ref):
    y = y_ref[...].astype(F32)
    yn = y * lax.rsqrt(jnp.mean(y * y, axis=-1, keepdims=True) + EPS) * gp_ref[...]
    xn = x_ref[...] + yn
    xo_ref[...] = xn
    hn = xn * lax.rsqrt(jnp.mean(xn * xn, axis=-1, keepdims=True) + EPS)
    ho_ref[...] = (hn * gn_ref[...]).astype(ho_ref.dtype)


def _resid_kernel(x_ref, y_ref, gp_ref, xo_ref):
    y = y_ref[...].astype(F32)
    yn = y * lax.rsqrt(jnp.mean(y * y, axis=-1, keepdims=True) + EPS) * gp_ref[...]
    xo_ref[...] = x_ref[...] + yn


def _resid_norm(x2, y, g_post, g_next):
    t, d = x2.shape
    tr = _tile(t, 256)
    row = pl.BlockSpec((tr, d), lambda i: (i, 0))
    vec = pl.BlockSpec((1, d), lambda i: (0, 0))
    if g_next is None:
        return pl.pallas_call(
            _resid_kernel,
            grid=(t // tr,),
            in_specs=[row, row, vec],
            out_specs=row,
            out_shape=jax.ShapeDtypeStruct((t, d), F32),
            compiler_params=_cparams("parallel"),
            name="resid",
        )(x2, y, g_post.reshape(1, d)), None
    return pl.pallas_call(
        _resid_norm_kernel,
        grid=(t // tr,),
        in_specs=[row, row, vec, vec],
        out_specs=[row, row],
        out_shape=[jax.ShapeDtypeStruct((t, d), F32), jax.ShapeDtypeStruct((t, d), BF16)],
        compiler_params=_cparams("parallel"),
        name="resid_norm",
    )(x2, y, g_post.reshape(1, d), g_next.reshape(1, d))


def _weight_window(w_hbm, layer, col, tn, k_minor):
    if k_minor:
        return w_hbm.at[pl.ds(col, tn), layer, :]
    return w_hbm.at[layer, :, pl.ds(pl.multiple_of(col, LANE), tn)]


def _load_weight_tile(w_hbm, stage, wb, sem, *, layer, col0, tn, nj, k_minor):
    j = pl.program_id(0)

    def copy(jj):
        return pltpu.make_async_copy(_weight_window(w_hbm, layer, col0 + jj * tn, tn, k_minor), stage, sem)

    @pl.when(j == 0)
    def _():
        copy(0).start()

    copy(j).wait()
    wb[...] = stage[...].astype(BF16)

    @pl.when(j + 1 < nj)
    def _():
        copy(j + 1).start()


def _tile_dot(h, wb, g0, gw, k_minor):
    if k_minor:
        return _dot_nt(h, wb[g0:g0 + gw, :])
    return _dot(h, wb[:, g0:g0 + gw])


def _wsmm_kernel(h_ref, w_hbm, o_ref, stage, wb, sem, *, tn, k_minor, **tile):
    @pl.when(pl.program_id(1) == 0)
    def _():
        _load_weight_tile(w_hbm, stage, wb, sem, tn=tn, k_minor=k_minor, **tile)

    h = h_ref[...]
    gw = _group_width(tn)
    for g0 in range(0, tn, gw):
        o_ref[:, g0:g0 + gw] = _tile_dot(h, wb, g0, gw, k_minor).astype(o_ref.dtype)


def _causal_conv(x, tail_ref, g0, w_ref, taps):
    tm, gw = x.shape
    xp = jnp.concatenate([tail_ref[:, g0:g0 + gw], x], axis=0)
    tail_ref[:, g0:g0 + gw] = x[tm - SUBLANE:, :]
    acc = w_ref[taps - 1:taps, g0:g0 + gw] * x
    for d in range(1, taps):
        acc = acc + w_ref[taps - 1 - d:taps - d, g0:g0 + gw] * pltpu.roll(xp, d, 0)[SUBLANE:, :]
    return acc


def _wsmm_conv_kernel(h_ref, w_hbm, cw_ref, o_ref, stage, wb, sem, tail, *, tn, k_minor, rows_per_seq, l2_tiles, **tile):
    j = pl.program_id(0)
    i = pl.program_id(1)
    tm = h_ref.shape[0]

    @pl.when(i == 0)
    def _():
        _load_weight_tile(w_hbm, stage, wb, sem, tn=tn, k_minor=k_minor, **tile)

    @pl.when((i * tm) % rows_per_seq == 0)
    def _():
        tail[...] = jnp.zeros_like(tail)

    gw = _group_width(tn)
    tr = tm // EPILOGUE_ROW_SPLIT

    def products(r0):
        h = h_ref[r0:r0 + tr, :]
        return [_tile_dot(h, wb, g0, gw, k_minor) for g0 in range(0, tn, gw)]

    def epilogue(r0, raw):
        for n, g0 in enumerate(range(0, tn, gw)):
            y = _silu(_causal_conv(raw[n], tail, g0, cw_ref, DN_CONV))
            for s0 in range(0, gw, DN_DK):
                ys = y[:, s0:s0 + DN_DK]
                yn = ys * lax.rsqrt(jnp.sum(ys * ys, axis=-1, keepdims=True) + EPS)
                o_ref[r0:r0 + tr, g0 + s0:g0 + s0 + DN_DK] = jnp.where(j < l2_tiles, yn, ys).astype(o_ref.dtype)

    pending = None
    for r0 in range(0, tm, tr):
        raw = products(r0)
        if pending is not None:
            epilogue(*pending)
        pending = (r0, raw)
    epilogue(*pending)


def _wsmm(h, w, layer, col0, n, out_dtype, *, tn, k_minor, name, conv=None, seq=None, l2_cols=0):
    t, k = h.shape
    tm = _tile(t, ROW_TILE if seq is None else min(ROW_TILE, seq))
    tn = _tile(n, tn)
    nj = n // tn
    tile = dict(layer=layer, col0=col0, nj=nj)
    wshape = (tn, k) if k_minor else (k, tn)
    in_specs = [pl.BlockSpec((tm, k), lambda j, i: (i, 0)), pl.BlockSpec(memory_space=pl.ANY)]
    scratch = [pltpu.VMEM(wshape, F32), pltpu.VMEM(wshape, BF16), pltpu.SemaphoreType.DMA(())]
    args = [h, w]
    if conv is None:
        body = functools.partial(_wsmm_kernel, tn=tn, k_minor=k_minor, **tile)
    else:
        assert seq % tm == 0 and l2_cols % tn == 0
        body = functools.partial(_wsmm_conv_kernel, tn=tn, k_minor=k_minor, rows_per_seq=seq, l2_tiles=l2_cols // tn, **tile)
        in_specs.append(pl.BlockSpec((conv.shape[0], tn), lambda j, i: (0, j)))
        scratch.append(pltpu.VMEM((SUBLANE, tn), F32))
        args.append(conv)
    return pl.pallas_call(
        body,
        grid=(nj, t // tm),
        in_specs=in_specs,
        out_specs=pl.BlockSpec((tm, tn), lambda j, i: (i, j)),
        out_shape=jax.ShapeDtypeStruct((t, n), out_dtype),
        scratch_shapes=scratch,
        compiler_params=_cparams("arbitrary", "arbitrary"),
        name=name,
    )(*args)


def _gates_kernel(h_ref, wt_hbm, g_ref, gt_ref, stage, wg, sem, *, layer):
    @pl.when(pl.program_id(0) == 0)
    def _():
        stage[...] = jnp.zeros_like(stage)
        copies = [pltpu.make_async_copy(wt_hbm.at[pl.ds(src, rows), layer, :], stage.at[pl.ds(dst, rows), :], sem.at[n])
                  for n, (src, dst, rows) in enumerate(GATE_SRC)]
        for cp in copies:
            cp.start()
        for cp in copies:
            cp.wait()
        wg[...] = stage[...].astype(BF16)

    gt = _dot_nt(wg[...], h_ref[...])
    gt_ref[...] = gt
    g_ref[...] = gt.T


def _gates(h, wt, layer):
    t, k = h.shape
    tm = _tile(t, 512)
    return pl.pallas_call(
        functools.partial(_gates_kernel, layer=layer),
        grid=(t // tm,),
        in_specs=[pl.BlockSpec((tm, k), lambda i: (i, 0)), pl.BlockSpec(memory_space=pl.ANY)],
        out_specs=[pl.BlockSpec((tm, LANE), lambda i: (i, 0)), pl.BlockSpec((LANE, tm), lambda i: (0, i))],
        out_shape=[jax.ShapeDtypeStruct((t, LANE), F32), jax.ShapeDtypeStruct((LANE, t), F32)],
        scratch_shapes=[pltpu.VMEM((LANE, k), F32), pltpu.VMEM((LANE, k), BF16), pltpu.SemaphoreType.DMA((len(GATE_SRC),))],
        compiler_params=_cparams("arbitrary"),
        name="gates",
    )(h, wt)


def _cumsum_mats(n, blk):
    r = _iota2((n, n), 0)
    c = _iota2((n, n), 1)
    same = (r // blk) == (c // blk)
    lower = jnp.where(same & (c <= r), 1.0, 0.0).astype(F32)
    upper = jnp.where(same & (r <= c), 1.0, 0.0).astype(F32)
    return lower, upper


def _rms_head(x, w_row):
    return x * lax.rsqrt(jnp.mean(x * x, axis=-1, keepdims=True) + EPS) * w_row


def _mlstm_kernel(p_ref, g_ref, gt_ref, bcol_ref, brow_ref, nw_ref, o_ref, c_scr, m_scr, *, lb):
    nh, dk, dv = M_HEADS, M_DK, M_DV
    scale = dk ** -0.5

    @pl.when(pl.program_id(1) == 0)
    def _():
        c_scr[...] = jnp.zeros_like(c_scr)
        m_scr[...] = jnp.zeros_like(m_scr)

    gcol = g_ref[...] + bcol_ref[...]
    fcol = _log_sigmoid(gcol)
    grow = gt_ref[0:2 * SUBLANE, :] + brow_ref[0:2 * SUBLANE, :]
    frow = _log_sigmoid(grow)
    nch = lb // CHUNK
    lower, upper = _cumsum_mats(lb, CHUNK)
    r = _iota2((lb, lb), 0)
    c = _iota2((lb, lb), 1)
    same = (r // CHUNK) == (c // CHUNK)
    causal = same & (c <= r)
    ones_bd = jnp.where(same, 1.0, 0.0).astype(F32)
    bcum_col = _sum01(lower, fcol)
    btot_col = _sum01(ones_bd, fcol)
    bcum_row = _sum01_t(frow, upper)
    one_col = jnp.where(_iota2((lb, LANE), 1) == 0, 1.0, 0.0).astype(F32)

    cols = []
    for h in range(nh):
        b_c = bcum_col[:, LN_MF + h:LN_MF + h + 1]
        bt_c = btot_col[:, LN_MF + h:LN_MF + h + 1]
        log_w = bt_c - b_c + gcol[:, LN_MI + h:LN_MI + h + 1]
        m = m_scr[h:h + 1, 0:1]
        mp_rows, mn_rows = [], []
        for ch in range(nch):
            r0 = ch * CHUNK
            mp_rows.append(jnp.broadcast_to(m, (CHUNK, 1)))
            m = jnp.maximum(bt_c[r0:r0 + 1, :] + m, jnp.max(log_w[r0:r0 + CHUNK, :], axis=0, keepdims=True))
            mn_rows.append(jnp.broadcast_to(m, (CHUNK, 1)))
        m_scr[h:h + 1, :] = jnp.broadcast_to(m, (1, LANE))
        cols.append((b_c, bt_c, log_w, jnp.concatenate(mp_rows, axis=0), jnp.concatenate(mn_rows, axis=0)))

    ss, w_inters, ems = [], [], []
    for h in range(nh):
        q = p_ref[:, h * dk:(h + 1) * dk]
        k = p_ref[:, nh * dk + h * dk:nh * dk + (h + 1) * dk]
        b_c, bt_c, log_w, mp_col, mn_col = cols[h]
        b_r = bcum_row[LN_MF + h:LN_MF + h + 1, :]
        i_r = grow[LN_MI + h:LN_MI + h + 1, :]
        log_d = jnp.where(causal, b_c - b_r + i_r, -jnp.inf)
        m_inter = b_c + mp_col
        m_t = jnp.maximum(m_inter, jnp.max(log_d, axis=-1, keepdims=True))
        ss.append((_dot_nt(q, k) * (scale * jnp.exp(log_d - m_t))).astype(BF16))
        w_inters.append(jnp.exp(m_inter - m_t) * scale)
        ems.append(jnp.exp(-m_t))

    intras, xs, cdecs = [], [], []
    for h in range(nh):
        v = p_ref[:, 2 * nh * dk + h * dv:2 * nh * dk + (h + 1) * dv]
        b_c, bt_c, log_w, mp_col, mn_col = cols[h]
        v_ext = jnp.concatenate([v.astype(F32), one_col], axis=1)
        intras.append(_dot(ss[h], v_ext.astype(BF16)))
        xs.append((jnp.exp(log_w - mn_col) * v_ext).astype(BF16))
        cdecs.append(jnp.exp(bt_c + mp_col - mn_col))

    inters = [[] for _ in range(nh)]
    for ch in range(nch):
        r0 = ch * CHUNK
        for h in range(nh):
            c_state = c_scr[h]
            inters[h].append(_dot(p_ref[r0:r0 + CHUNK, h * dk:(h + 1) * dk], c_state.astype(BF16)))
            k_c = p_ref[r0:r0 + CHUNK, nh * dk + h * dk:nh * dk + (h + 1) * dk]
            c_scr[h] = cdecs[h][r0:r0 + 1, :] * c_state + _dot_tn(k_c, xs[h][r0:r0 + CHUNK, :])

    for h in range(nh):
        og = p_ref[:, 2 * nh * dk + nh * dv + h * dv:2 * nh * dk + nh * dv + (h + 1) * dv]
        tot = w_inters[h] * jnp.concatenate(inters[h], axis=0) + intras[h]
        hh = tot[:, :dv] / jnp.maximum(jnp.abs(tot[:, dv:dv + 1]), ems[h])
        y = _sigmoid(og.astype(F32)) * _rms_head(hh, nw_ref[h:h + 1, :])
        o_ref[:, h * dv:(h + 1) * dv] = y.astype(o_ref.dtype)


def _mlstm(proj, gates, gates_t, gate_b, norm_w, bsz, seq):
    t = bsz * seq
    lb = _tile(seq, MIXER_ROWS)
    nb = seq // lb
    bias = jnp.zeros((LANE,), F32).at[LN_MI:LN_MI + 2 * M_HEADS].set(gate_b)
    return pl.pallas_call(
        functools.partial(_mlstm_kernel, lb=lb),
        grid=(bsz, nb),
        in_specs=[
            pl.BlockSpec((lb, M_W), lambda b, c: (b * nb + c, 0)),
            pl.BlockSpec((lb, LANE), lambda b, c: (b * nb + c, 0)),
            pl.BlockSpec((LANE, lb), lambda b, c: (0, b * nb + c)),
            pl.BlockSpec((1, LANE), lambda b, c: (0, 0)),
            pl.BlockSpec((LANE, 1), lambda b, c: (0, 0)),
            pl.BlockSpec((M_HEADS, M_DV), lambda b, c: (0, 0)),
        ],
        out_specs=pl.BlockSpec((lb, BRANCH_W), lambda b, c: (b * nb + c, 0)),
        out_shape=jax.ShapeDtypeStruct((t, BRANCH_W), BF16),
        scratch_shapes=[pltpu.VMEM((M_HEADS, M_DK, M_DV + LANE), F32), pltpu.VMEM((SUBLANE, LANE), F32)],
        compiler_params=_cparams("parallel", "arbitrary"),
        name="mlstm",
    )(proj, gates, gates_t, bias.reshape(1, LANE), bias.reshape(LANE, 1), norm_w)


def _unit_lower_inverses(a_list, blk_diag):
    n = a_list[0].shape[0]
    r = _iota2((n, n), 0)
    c = _iota2((n, n), 1)
    eye = jnp.where(r == c, 1.0, 0.0).astype(F32)
    base = SUBLANE
    in_base = (r // base) == (c // base)
    nd = [jnp.where(in_base, -a, 0.0).astype(BF16) for a in a_list]
    n2 = [_dot(m, m).astype(BF16) for m in nd]
    xs = [eye + m.astype(F32) for m in nd]
    xs = [x + _dot(x.astype(BF16), m) for x, m in zip(xs, n2)]
    n4 = [_dot(m, m).astype(BF16) for m in n2]
    xs = [x + _dot(x.astype(BF16), m) for x, m in zip(xs, n4)]
    blk = base
    while blk < blk_diag:
        sel = ((r // (2 * blk)) == (c // (2 * blk))) & ((r // blk) != (c // blk))
        offs = [jnp.where(sel, a, 0.0).astype(BF16) for a in a_list]
        xb = [x.astype(BF16) for x in xs]
        xa = [_dot(x, o).astype(BF16) for x, o in zip(xb, offs)]
        xs = [x - _dot(m, x_b) for x, m, x_b in zip(xs, xa, xb)]
        blk *= 2
    return xs


def _gdn_kernel(p_ref, z_ref, g_ref, gt_ref, pcol_ref, prow_ref, nw_ref, o_ref,
                s_scr, w_scr, u_scr, qg_scr, kd_scr, pm_scr, qs_scr, *, lb):
    nh, dk, dv = DN_HEADS, DN_DK, DN_DV
    scale = dk ** -0.5

    @pl.when(pl.program_id(1) == 0)
    def _():
        s_scr[...] = jnp.zeros_like(s_scr)

    gc = g_ref[...]
    beta_col = _sigmoid(gc)
    gl_col = -jnp.exp(pcol_ref[0:1, :]) * _softplus(gc + pcol_ref[1:2, :])
    row0 = LN_DA % SUBLANE
    rbase = LN_DA - row0
    gr = gt_ref[rbase:rbase + 2 * SUBLANE, :]
    pr = prow_ref[rbase:rbase + 2 * SUBLANE, :]
    gl_row = -jnp.exp(pr[:, 0:1]) * _softplus(gr + pr[:, 1:2])

    lower, upper = _cumsum_mats(lb, CHUNK)
    r = _iota2((lb, lb), 0)
    c = _iota2((lb, lb), 1)
    same = (r // CHUNK) == (c // CHUNK)
    incl = same & (c <= r)
    strict = same & (c < r)
    ones_bd = jnp.where(same, 1.0, 0.0).astype(F32)
    gcum_col = _sum01(lower, gl_col)
    gtot_col = _sum01(ones_bd, gl_col)
    gcum_row = _sum01_t(gl_row, upper)

    ks, qs, vs, a_list, gccs, bcs = [], [], [], [], [], []
    for h in range(nh):
        qa = p_ref[:, h * dk:(h + 1) * dk]
        ka = p_ref[:, nh * dk + h * dk:nh * dk + (h + 1) * dk]
        va = p_ref[:, 2 * nh * dk + h * dv:2 * nh * dk + (h + 1) * dv]
        gcc = gcum_col[:, LN_DA + h:LN_DA + h + 1]
        gcr = gcum_row[row0 + h:row0 + h + 1, :]
        b_c = beta_col[:, LN_DB + h:LN_DB + h + 1]
        decay = jnp.exp(jnp.where(incl, gcc - gcr, -jnp.inf))
        a_list.append(_dot_nt((ka.astype(F32) * b_c).astype(BF16), ka) * jnp.where(strict, decay, 0.0))
        pm_scr[h] = (_dot_nt(qa, ka) * (scale * decay)).astype(BF16)
        ks.append(ka.astype(F32)); qs.append(qa.astype(F32)); vs.append(va.astype(F32)); gccs.append(gcc); bcs.append(b_c)
    xs = _unit_lower_inverses(a_list, CHUNK)
    for h in range(nh):
        eg = jnp.exp(gccs[h])
        rhs = jnp.concatenate([ks[h] * (bcs[h] * eg), vs[h] * bcs[h]], axis=1).astype(BF16)
        wu = _dot(xs[h].astype(BF16), rhs)
        w_scr[h] = wu[:, :dk].astype(BF16)
        u_scr[h] = wu[:, dk:]
        qg_scr[h] = (qs[h] * (eg * scale)).astype(BF16)
        gtot = gtot_col[:, LN_DA + h:LN_DA + h + 1]
        kd_scr[h] = (ks[h] * jnp.exp(gtot - gccs[h])).astype(BF16)

    for ch in range(lb // CHUNK):
        r0 = ch * CHUNK
        dec_row = jnp.exp(gtot_col[r0:r0 + 1, :])
        for h in range(nh):
            s_prev = s_scr[h]
            sb = s_prev.astype(BF16)
            u_new = u_scr[h, r0:r0 + CHUNK, :] - _dot(w_scr[h, r0:r0 + CHUNK, :], sb)
            qs_scr[h, r0:r0 + CHUNK, :] = _dot(qg_scr[h, r0:r0 + CHUNK, :], sb)
            u_scr[h, r0:r0 + CHUNK, :] = u_new
            s_scr[h] = dec_row[:, LN_DA + h:LN_DA + h + 1] * s_prev + _dot_tn(kd_scr[h, r0:r0 + CHUNK, :], u_new.astype(BF16))

    for h in range(nh):
        o = qs_scr[h] + _dot(pm_scr[h], u_scr[h].astype(BF16))
        z = z_ref[:, h * dv:(h + 1) * dv].astype(F32)
        o_ref[:, h * dv:(h + 1) * dv] = (_rms_head(o, nw_ref[...]) * _silu(z)).astype(o_ref.dtype)


def _gdn(qkv, z, gates, gates_t, a_log, dt_bias, norm_w, bsz, seq):
    t = bsz * seq
    lb = _tile(seq, GDN_ROWS)
    nb = seq // lb
    nh = DN_HEADS
    par = jnp.zeros((2, LANE), F32)
    par = par.at[0, LN_DA:LN_DA + nh].set(a_log).at[1, LN_DA:LN_DA + nh].set(dt_bias)
    return pl.pallas_call(
        functools.partial(_gdn_kernel, lb=lb),
        grid=(bsz, nb),
        in_specs=[
            pl.BlockSpec((lb, DQKV_W), lambda b, c: (b * nb + c, 0)),
            pl.BlockSpec((lb, DZ_W), lambda b, c: (b * nb + c, 0)),
            pl.BlockSpec((lb, LANE), lambda b, c: (b * nb + c, 0)),
            pl.BlockSpec((LANE, lb), lambda b, c: (0, b * nb + c)),
            pl.BlockSpec((2, LANE), lambda b, c: (0, 0)),
            pl.BlockSpec((LANE, 2), lambda b, c: (0, 0)),
            pl.BlockSpec((1, DN_DV), lambda b, c: (0, 0)),
        ],
        out_specs=pl.BlockSpec((lb, BRANCH_W), lambda b, c: (b * nb + c, 0)),
        out_shape=jax.ShapeDtypeStruct((t, BRANCH_W), BF16),
        scratch_shapes=[
            pltpu.VMEM((nh, DN_DK, DN_DV), F32),
            pltpu.VMEM((nh, lb, DN_DK), BF16),
            pltpu.VMEM((nh, lb, DN_DV), F32),
            pltpu.VMEM((nh, lb, DN_DK), BF16),
            pltpu.VMEM((nh, lb, DN_DK), BF16),
            pltpu.VMEM((nh, lb, lb), BF16),
            pltpu.VMEM((nh, lb, DN_DV), F32),
        ],
        compiler_params=_cparams("parallel", "arbitrary"),
        name="gdn",
    )(qkv, z, gates, gates_t, par, par.T, norm_w.reshape(1, DN_DV))


def _gla_kernel(p_ref, g_ref, wa_ref, ba_ref, nw_ref, o_ref, s_scr, *, lb):
    nh, dk, dv = G_HEADS, G_DK, G_DV
    scale = dk ** -0.5
    grp = 64

    @pl.when(pl.program_id(1) == 0)
    def _():
        s_scr[...] = jnp.zeros_like(s_scr)

    log_a = _log_sigmoid(_dot(g_ref[...].astype(BF16), wa_ref[...].astype(BF16)) + ba_ref[...]) * (1.0 / G_TAU)
    lower, _ = _cumsum_mats(grp, G_CHUNK)
    r = _iota2((grp, grp), 0)
    c = _iota2((grp, grp), 1)
    same = (r // G_CHUNK) == (c // G_CHUNK)
    ones_blk = jnp.where(same, 1.0, 0.0).astype(F32)
    intra = same & (c <= r)

    for gi in range(lb // grp):
        r0 = gi * grp
        la = log_a[r0:r0 + grp, :]
        lc_all = _sum01(lower, la)
        le_all = _sum01(ones_blk, la)
        q_ins, kds, decs, vs, o_intras = [], [], [], [], []
        for h in range(nh):
            q = p_ref[r0:r0 + grp, h * dk:(h + 1) * dk].astype(F32)
            k = p_ref[r0:r0 + grp, nh * dk + h * dk:nh * dk + (h + 1) * dk].astype(F32)
            v = p_ref[r0:r0 + grp, 2 * nh * dk + h * dv:2 * nh * dk + (h + 1) * dv]
            lc = lc_all[:, h * dk:(h + 1) * dk]
            le = le_all[:, h * dk:(h + 1) * dk]
            q_in = (q * (jnp.exp(lc) * scale)).astype(BF16)
            k_in = (k * jnp.exp(-lc)).astype(BF16)
            p = jnp.where(intra, _dot_nt(q_in, k_in), 0.0)
            q_ins.append(q_in)
            kds.append((k * jnp.exp(le - lc)).astype(BF16))
            decs.append(jnp.exp(le))
            vs.append(v)
            o_intras.append(_dot(p.astype(BF16), v))
        outs = [[] for _ in range(nh)]
        for ci in range(grp // G_CHUNK):
            c0 = ci * G_CHUNK
            for h in range(nh):
                st = s_scr[h]
                outs[h].append(_dot_nt(q_ins[h][c0:c0 + G_CHUNK], st.astype(BF16)))
                s_scr[h] = decs[h][c0:c0 + 1, :] * st + _dot_tn(vs[h][c0:c0 + G_CHUNK], kds[h][c0:c0 + G_CHUNK])
        for h in range(nh):
            rg = p_ref[r0:r0 + grp, 2 * nh * dk + nh * dv + h * dv:2 * nh * dk + nh * dv + (h + 1) * dv]
            o = jnp.concatenate(outs[h], axis=0) + o_intras[h]
            y = _rms_head(o, nw_ref[...]) * _silu(rg.astype(F32))
            o_ref[r0:r0 + grp, h * dv:(h + 1) * dv] = y.astype(o_ref.dtype)


def _gla(proj, gates, w_a2, b_a, norm_w, bsz, seq):
    t = bsz * seq
    lb = _tile(seq, GLA_ROWS)
    nb = seq // lb
    gw = G_HEADS * G_DK
    wa = jnp.zeros((LANE, gw), F32).at[LN_GLR:LN_GLR + G_RANK, :].set(w_a2)
    return pl.pallas_call(
        functools.partial(_gla_kernel, lb=lb),
        grid=(bsz, nb),
        in_specs=[
            pl.BlockSpec((lb, G_W), lambda b, c: (b * nb + c, 0)),
            pl.BlockSpec((lb, LANE), lambda b, c: (b * nb + c, 0)),
            pl.BlockSpec((LANE, gw), lambda b, c: (0, 0)),
            pl.BlockSpec((1, gw), lambda b, c: (0, 0)),
            pl.BlockSpec((1, G_DV), lambda b, c: (0, 0)),
        ],
        out_specs=pl.BlockSpec((lb, BRANCH_W), lambda b, c: (b * nb + c, 0)),
        out_shape=jax.ShapeDtypeStruct((t, BRANCH_W), BF16),
        scratch_shapes=[pltpu.VMEM((G_HEADS, G_DV, G_DK), F32)],
        compiler_params=_cparams("parallel", "arbitrary"),
        name="gla",
    )(proj, gates, wa, b_a.reshape(1, gw), norm_w.reshape(1, G_DV))


def _merge_kernel(ym_ref, yd_ref, yg_ref, gm_ref, gd_ref, gg_ref, w_hbm, o_ref, stage, wb, sem, *, tn, layer, nj):
    tm = o_ref.shape[0]

    @pl.when(pl.program_id(1) == 0)
    def _():
        for g in range(N_BRANCH):
            _load_weight_tile(w_hbm.at[layer], stage.at[g], wb.at[g], sem.at[g], layer=g, col0=0, tn=tn, nj=nj, k_minor=False)

    y_refs = (ym_ref, yd_ref, yg_ref)
    g_refs = (gm_ref, gd_ref, gg_ref)
    gw = _group_width(tn)
    tr = max(tm // MERGE_ROW_SPLIT, SUBLANE)

    def products(r0):
        return [[_dot(y_refs[g][r0:r0 + tr, :], wb[g, :, g0:g0 + gw]) for g in range(N_BRANCH)] for g0 in range(0, tn, gw)]

    def epilogue(r0, raw):
        for n, g0 in enumerate(range(0, tn, gw)):
            acc = _sigmoid(g_refs[0][r0:r0 + tr, g0:g0 + gw].astype(F32)) * raw[n][0]
            for g in range(1, N_BRANCH):
                acc = acc + _sigmoid(g_refs[g][r0:r0 + tr, g0:g0 + gw].astype(F32)) * raw[n][g]
            o_ref[r0:r0 + tr, g0:g0 + gw] = acc.astype(o_ref.dtype)

    pending = None
    for r0 in range(0, tm, tr):
        raw = products(r0)
        if pending is not None:
            epilogue(*pending)
        pending = (r0, raw)
    epilogue(*pending)


def _merge(y_m, y_d, y_g, w_branch, layer, merge_pre, d_model):
    t = y_m.shape[0]
    tm = _tile(t, ROW_TILE)
    tn = _tile(d_model, COL_TILE_FFN)
    nj = d_model // tn
    yspec = pl.BlockSpec((tm, BRANCH_W), lambda j, i: (i, 0))
    gspec = lambda g: pl.BlockSpec((tm, tn), lambda j, i: (i, g * nj + j))
    return pl.pallas_call(
        functools.partial(_merge_kernel, tn=tn, layer=layer, nj=nj),
        grid=(nj, t // tm),
        in_specs=[yspec, yspec, yspec, gspec(0), gspec(1), gspec(2), pl.BlockSpec(memory_space=pl.ANY)],
        out_specs=pl.BlockSpec((tm, tn), lambda j, i: (i, j)),
        out_shape=jax.ShapeDtypeStruct((t, d_model), BF16),
        scratch_shapes=[pltpu.VMEM((N_BRANCH, BRANCH_W, tn), F32), pltpu.VMEM((N_BRANCH, BRANCH_W, tn), BF16),
                        pltpu.SemaphoreType.DMA((N_BRANCH,))],
        compiler_params=_cparams("arbitrary", "arbitrary"),
        name="merge",
    )(y_m, y_d, y_g, merge_pre, merge_pre, merge_pre, w_branch)


def _ffn_up_kernel(h_ref, w_hbm, cg_ref, cu_ref, o_ref, stage, wb, sem, tail_g, tail_u, *, tn, d_ff, rows_per_seq, layer, nj):
    i = pl.program_id(1)
    tm = h_ref.shape[0]

    @pl.when(i == 0)
    def _():
        for m, col0 in enumerate((0, d_ff)):
            _load_weight_tile(w_hbm, stage.at[m], wb.at[m], sem.at[m], layer=layer, col0=col0, tn=tn, nj=nj, k_minor=False)

    @pl.when((i * tm) % rows_per_seq == 0)
    def _():
        tail_g[...] = jnp.zeros_like(tail_g)
        tail_u[...] = jnp.zeros_like(tail_u)

    gw = _group_width(tn)
    tr = max(tm // FFN_ROW_SPLIT, SUBLANE)

    def products(r0):
        h = h_ref[r0:r0 + tr, :]
        return [(_dot(h, wb[0, :, g0:g0 + gw]), _dot(h, wb[1, :, g0:g0 + gw])) for g0 in range(0, tn, gw)]

    def epilogue(r0, raw):
        for n, g0 in enumerate(range(0, tn, gw)):
            gate = _causal_conv(raw[n][0], tail_g, g0, cg_ref, FFN_CONV)
            up = _causal_conv(raw[n][1], tail_u, g0, cu_ref, FFN_CONV)
            o_ref[r0:r0 + tr, g0:g0 + gw] = (_silu(gate) * up).astype(o_ref.dtype)

    pending = None
    for r0 in range(0, tm, tr):
        raw = products(r0)
        if pending is not None:
            epilogue(*pending)
        pending = (r0, raw)
    epilogue(*pending)


def _ffn_up(h, w_up, conv_w, layer, seq):
    t, k = h.shape
    d_ff = w_up.shape[-1] // 2
    tm = _tile(seq, ROW_TILE)
    tn = _tile(d_ff, COL_TILE_FFN)
    nj = d_ff // tn
    return pl.pallas_call(
        functools.partial(_ffn_up_kernel, tn=tn, d_ff=d_ff, rows_per_seq=seq, layer=layer, nj=nj),
        grid=(nj, t // tm),
        in_specs=[
            pl.BlockSpec((tm, k), lambda j, i: (i, 0)),
            pl.BlockSpec(memory_space=pl.ANY),
            pl.BlockSpec((None, FFN_CONV, tn), lambda j, i: (layer, 0, j)),
            pl.BlockSpec((None, FFN_CONV, tn), lambda j, i: (layer, 0, nj + j)),
        ],
        out_specs=pl.BlockSpec((tm, tn), lambda j, i: (i, j)),
        out_shape=jax.ShapeDtypeStruct((t, d_ff), BF16),
        scratch_shapes=[pltpu.VMEM((2, k, tn), F32), pltpu.VMEM((2, k, tn), BF16), pltpu.SemaphoreType.DMA((2,)),
                        pltpu.VMEM((SUBLANE, tn), F32), pltpu.VMEM((SUBLANE, tn), F32)],
        compiler_params=_cparams("arbitrary", "arbitrary"),
        name="ffn_up",
    )(h, w_up, conv_w, conv_w)


def kernel(x, norm_mix_pre, norm_mix_post, norm_ffn_pre, norm_ffn_post, w_in, mlstm_gate_b, mlstm_norm, dn_conv, dn_a_log, dn_dt_bias, dn_norm, gla_w_a2, gla_b_a, gla_norm, w_branch, w_out, w_ffn_up, ffn_conv, w_ffn_down):
    bsz, seq, d_model = x.shape
    depth = w_in.shape[0]
    t = bsz * seq
    x2 = x.reshape(t, d_model)
    wt = jnp.transpose(w_in, (2, 0, 1))
    proj = functools.partial(_wsmm, k_minor=True)
    dense = functools.partial(_wsmm, k_minor=False)
    h = _norm_cast(x2, norm_mix_pre[0])
    for l in range(depth):
        proj_m = proj(h, wt, l, OFF_M, M_W, BF16, tn=COL_TILE_SLAB, name="proj_m")
        d_qkv = proj(h, wt, l, OFF_D, DQKV_W, BF16, tn=COL_TILE_SLAB, name="proj_dqkv",
                     conv=dn_conv[l], seq=seq, l2_cols=2 * DN_HEADS * DN_DK)
        d_z = proj(h, wt, l, OFF_DZ, DZ_W, BF16, tn=COL_TILE_SLAB, name="proj_dz")
        proj_g = proj(h, wt, l, OFF_G, G_W, BF16, tn=COL_TILE_SLAB, name="proj_g")
        merge_pre = proj(h, wt, l, OFF_MERGE, N_BRANCH * d_model, BF16, tn=COL_TILE_WIDE, name="proj_merge")
        gates, gates_t = _gates(h, wt, l)
        y_m = _mlstm(proj_m, gates, gates_t, mlstm_gate_b[l], mlstm_norm[l], bsz, seq)
        y_d = _gdn(d_qkv, d_z, gates, gates_t, dn_a_log[l], dn_dt_bias[l], dn_norm[l], bsz, seq)
        y_g = _gla(proj_g, gates, gla_w_a2[l], gla_b_a[l], gla_norm[l], bsz, seq)
        merged = _merge(y_m, y_d, y_g, w_branch, l, merge_pre, d_model)
        mix = dense(merged, w_out, l, 0, d_model, BF16, tn=COL_TILE_WIDE, name="out_proj")
        x2, h = _resid_norm(x2, mix, norm_mix_post[l], norm_ffn_pre[l])
        act = _ffn_up(h, w_ffn_up, ffn_conv, l, seq)
        down = dense(act, w_ffn_down, l, 0, d_model, BF16, tn=COL_TILE_FFN, name="ffn_down")
        x2, h = _resid_norm(x2, down, norm_ffn_post[l], norm_mix_pre[l + 1] if l + 1 < depth else None)
    return x2.reshape(bsz, seq, d_model)
```

```python
import functools

import jax
import jax.numpy as jnp
from jax import lax
from jax.experimental import pallas as pl
from jax.experimental.pallas import tpu as pltpu

F32 = jnp.float32
BF16 = jnp.bfloat16
EPS = 1e-6
LANE = 128
SUBLANE = 8
MXU_WIDTH = 256
VMEM_LIMIT = 56 * 1024 * 1024

M_HEADS, M_DK, M_DV = 6, 128, 256
DN_HEADS, DN_DK, DN_DV, DN_CONV = 12, 128, 128, 4
G_HEADS, G_DK, G_DV, G_RANK, G_TAU = 6, 128, 256, 16, 16.0
CHUNK, G_CHUNK = 64, 16
N_BRANCH, BRANCH_W, FFN_CONV = 3, 1536, 3

M_W = 2 * M_HEADS * M_DK + 2 * M_HEADS * M_DV
DQKV_W = 2 * DN_HEADS * DN_DK + DN_HEADS * DN_DV
DZ_W = DN_HEADS * DN_DV
G_W = 2 * G_HEADS * G_DK + 2 * G_HEADS * G_DV
OFF_M = 0
OFF_MGATE = OFF_M + M_W
OFF_D = OFF_MGATE + 2 * M_HEADS
OFF_DZ = OFF_D + DQKV_W
OFF_DGATE = OFF_DZ + DZ_W
OFF_G = OFF_DGATE + 2 * DN_HEADS
OFF_GLR = OFF_G + G_W
OFF_MERGE = OFF_GLR + G_RANK
GATE_SRC = ((OFF_MGATE, 0, 2 * SUBLANE), (OFF_DGATE, 2 * SUBLANE, 2 * DN_HEADS), (OFF_GLR, 2 * SUBLANE + 2 * DN_HEADS, G_RANK))
LN_MI = 0
LN_MF = LN_MI + M_HEADS
LN_DB = GATE_SRC[1][1]
LN_DA = LN_DB + DN_HEADS
LN_GLR = GATE_SRC[2][1]
assert 2 * M_HEADS <= GATE_SRC[0][2] and LN_GLR + G_RANK <= LANE

ROW_TILE = 1024
COL_TILE_SLAB = 768
COL_TILE_WIDE = 1024
COL_TILE_FFN = 512
EPILOGUE_ROW_SPLIT = 4
FFN_ROW_SPLIT = 8
MERGE_ROW_SPLIT = 4
MIXER_ROWS = 256
GDN_ROWS = 128
GLA_ROWS = 256
GLA_GROUP = 64


def _cparams(*sem):
    return pltpu.CompilerParams(dimension_semantics=sem, vmem_limit_bytes=VMEM_LIMIT)


def _dot(a, b):
    return jnp.dot(a, b, preferred_element_type=F32)


def _dot_nt(a, b):
    return lax.dot_general(a, b, (((1,), (1,)), ((), ())), preferred_element_type=F32)


def _dot_tn(a, b):
    return lax.dot_general(a, b, (((0,), (0,)), ((), ())), preferred_element_type=F32)


def _split3(x):
    hi = x.astype(BF16)
    r1 = x - hi.astype(F32)
    mid = r1.astype(BF16)
    lo = (r1 - mid.astype(F32)).astype(BF16)
    return hi, mid, lo


def _sum01(m01, x):
    m = m01.astype(BF16)
    hi, mid, lo = _split3(x)
    return _dot(m, hi) + _dot(m, mid) + _dot(m, lo)


def _sum01_t(x, m01):
    m = m01.astype(BF16)
    hi, mid, lo = _split3(x)
    return _dot(hi, m) + _dot(mid, m) + _dot(lo, m)


def _sigmoid(x):
    return 1.0 / (1.0 + jnp.exp(-x))


def _silu(x):
    return x * _sigmoid(x)


def _log_sigmoid(x):
    return jnp.minimum(x, 0.0) - jnp.log(1.0 + jnp.exp(-jnp.abs(x)))


def _softplus(x):
    return jnp.maximum(x, 0.0) + jnp.log(1.0 + jnp.exp(-jnp.abs(x)))


def _iota2(shape, axis):
    return lax.broadcasted_iota(jnp.int32, shape, axis)


def _tile(n, pref):
    t = min(n, pref)
    assert n % t == 0, (n, pref)
    return t


def _group_width(tn):
    return MXU_WIDTH if tn % MXU_WIDTH == 0 else tn


def _norm_cast_kernel(x_ref, g_ref, o_ref):
    x = x_ref[...]
    y = x * lax.rsqrt(jnp.mean(x * x, axis=-1, keepdims=True) + EPS)
    o_ref[...] = (y * g_ref[...]).astype(o_ref.dtype)


def _norm_cast(x2, g):
    t, d = x2.shape
    tr = _tile(t, 256)
    return pl.pallas_call(
        _norm_cast_kernel,
        grid=(t // tr,),
        in_specs=[pl.BlockSpec((tr, d), lambda i: (i, 0)), pl.BlockSpec((1, d), lambda i: (0, 0))],
        out_specs=pl.BlockSpec((tr, d), lambda i: (i, 0)),
        out_shape=jax.ShapeDtypeStruct((t, d), BF16),
        compiler_params=_cparams("parallel"),
        name="norm_cast",
    )(x2, g.reshape(1, d))


def _resid_norm_kernel(x_ref, y_ref, gp_ref, gn_ref, xo_ref, ho_ref):
    y = y_ref[...].astype(F32)
    yn = y * lax.rsqrt(jnp.mean(y * y, axis=-1, keepdims=True) + EPS) * gp_ref[...]
    xn = x_ref[...] + yn
    xo_ref[...] = xn
    hn = xn * lax.rsqrt(jnp.mean(xn * xn, axis=-1, keepdims=True) + EPS)
    ho_ref[...] = (hn * gn_ref[...]).astype(ho_ref.dtype)


def _resid_kernel(x_ref, y_ref, gp_ref, xo_ref):
    y = y_ref[...].astype(F32)
    yn = y * lax.rsqrt(jnp.mean(y * y, axis=-1, keepdims=True) + EPS) * gp_ref[...]
    xo_ref[...] = x_ref[...] + yn


def _resid_norm(x2, y, g_post, g_next):
    t, d = x2.shape
    tr = _tile(t, 256)
    row = pl.BlockSpec((tr, d), lambda i: (i, 0))
    vec = pl.BlockSpec((1, d), lambda i: (0, 0))
    if g_next is None:
        return pl.pallas_call(
            _resid_kernel,
            grid=(t // tr,),
            in_specs=[row, row, vec],
            out_specs=row,
            out_shape=jax.ShapeDtypeStruct((t, d), F32),
            compiler_params=_cparams("parallel"),
            name="resid",
        )(x2, y, g_post.reshape(1, d)), None
    return pl.pallas_call(
        _resid_norm_kernel,
        grid=(t // tr,),
        in_specs=[row, row, vec, vec],
        out_specs=[row, row],
        out_shape=[jax.ShapeDtypeStruct((t, d), F32), jax.ShapeDtypeStruct((t, d), BF16)],
        compiler_params=_cparams("parallel"),
        name="resid_norm",
    )(x2, y, g_post.reshape(1, d), g_next.reshape(1, d))


def _weight_window(w_hbm, layer, col, tn, k_minor):
    if k_minor:
        return w_hbm.at[pl.ds(col, tn), layer, :]
    return w_hbm.at[layer, :, pl.ds(pl.multiple_of(col, LANE), tn)]


def _load_weight_tile(w_hbm, stage, wb, sem, *, layer, col0, tn, nj, k_minor):
    j = pl.program_id(0)

    def copy(jj):
        return pltpu.make_async_copy(_weight_window(w_hbm, layer, col0 + jj * tn, tn, k_minor), stage, sem)

    @pl.when(j == 0)
    def _():
        copy(0).start()

    copy(j).wait()
    wb[...] = stage[...].astype(BF16)

    @pl.when(j + 1 < nj)
    def _():
        copy(j + 1).start()


def _tile_dot(h, wb, g0, gw, k_minor):
    if k_minor:
        return _dot_nt(h, wb[g0:g0 + gw, :])
    return _dot(h, wb[:, g0:g0 + gw])


def _wsmm_kernel(h_ref, w_hbm, o_ref, stage, wb, sem, *, tn, k_minor, **tile):
    @pl.when(pl.program_id(1) == 0)
    def _():
        _load_weight_tile(w_hbm, stage, wb, sem, tn=tn, k_minor=k_minor, **tile)

    h = h_ref[...]
    gw = _group_width(tn)
    for g0 in range(0, tn, gw):
        o_ref[:, g0:g0 + gw] = _tile_dot(h, wb, g0, gw, k_minor).astype(o_ref.dtype)


def _causal_conv(x, tail_ref, g0, w_ref, taps):
    tm, gw = x.shape
    xp = jnp.concatenate([tail_ref[:, g0:g0 + gw], x], axis=0)
    tail_ref[:, g0:g0 + gw] = x[tm - SUBLANE:, :]
    acc = w_ref[taps - 1:taps, g0:g0 + gw] * x
    for d in range(1, taps):
        acc = acc + w_ref[taps - 1 - d:taps - d, g0:g0 + gw] * pltpu.roll(xp, d, 0)[SUBLANE:, :]
    return acc


def _wsmm_conv_kernel(h_ref, w_hbm, cw_ref, o_ref, stage, wb, sem, tail, *, tn, k_minor, rows_per_seq, l2_tiles, **tile):
    j = pl.program_id(0)
    i = pl.program_id(1)
    tm = h_ref.shape[0]

    @pl.when(i == 0)
    def _():
        _load_weight_tile(w_hbm, stage, wb, sem, tn=tn, k_minor=k_minor, **tile)

    @pl.when((i * tm) % rows_per_seq == 0)
    def _():
        tail[...] = jnp.zeros_like(tail)

    gw = _group_width(tn)
    tr = tm // EPILOGUE_ROW_SPLIT

    def products(r0):
        h = h_ref[r0:r0 + tr, :]
        return [_tile_dot(h, wb, g0, gw, k_minor) for g0 in range(0, tn, gw)]

    def epilogue(r0, raw):
        for n, g0 in enumerate(range(0, tn, gw)):
            y = _silu(_causal_conv(raw[n], tail, g0, cw_ref, DN_CONV))
            for s0 in range(0, gw, DN_DK):
                ys = y[:, s0:s0 + DN_DK]
                yn = ys * lax.rsqrt(jnp.sum(ys * ys, axis=-1, keepdims=True) + EPS)
                o_ref[r0:r0 + tr, g0 + s0:g0 + s0 + DN_DK] = jnp.where(j < l2_tiles, yn, ys).astype(o_ref.dtype)

    pending = None
    for r0 in range(0, tm, tr):
        raw = products(r0)
        if pending is not None:
            epilogue(*pending)
        pending = (r0, raw)
    epilogue(*pending)


def _wsmm(h, w, layer, col0, n, out_dtype, *, tn, k_minor, name, conv=None, seq=None, l2_cols=0):
    t, k = h.shape
    tm = _tile(t, ROW_TILE if seq is None else min(ROW_TILE, seq))
    tn = _tile(n, tn)
    nj = n // tn
    tile = dict(layer=layer, col0=col0, nj=nj)
    wshape = (tn, k) if k_minor else (k, tn)
    in_specs = [pl.BlockSpec((tm, k), lambda j, i: (i, 0)), pl.BlockSpec(memory_space=pl.ANY)]
    scratch = [pltpu.VMEM(wshape, F32), pltpu.VMEM(wshape, BF16), pltpu.SemaphoreType.DMA(())]
    args = [h, w]
    if conv is None:
        body = functools.partial(_wsmm_kernel, tn=tn, k_minor=k_minor, **tile)
    else:
        assert seq % tm == 0 and l2_cols % tn == 0
        body = functools.partial(_wsmm_conv_kernel, tn=tn, k_minor=k_minor, rows_per_seq=seq, l2_tiles=l2_cols // tn, **tile)
        in_specs.append(pl.BlockSpec((conv.shape[0], tn), lambda j, i: (0, j)))
        scratch.append(pltpu.VMEM((SUBLANE, tn), F32))
        args.append(conv)
    return pl.pallas_call(
        body,
        grid=(nj, t // tm),
        in_specs=in_specs,
        out_specs=pl.BlockSpec((tm, tn), lambda j, i: (i, j)),
        out_shape=jax.ShapeDtypeStruct((t, n), out_dtype),
        scratch_shapes=scratch,
        compiler_params=_cparams("arbitrary", "arbitrary"),
        name=name,
    )(*args)


def _gates_kernel(h_ref, wt_hbm, g_ref, gt_ref, stage, wg, sem, *, layer):
    @pl.when(pl.program_id(0) == 0)
    def _():
        stage[...] = jnp.zeros_like(stage)
        copies = [pltpu.make_async_copy(wt_hbm.at[pl.ds(src, rows), layer, :], stage.at[pl.ds(dst, rows), :], sem.at[n])
                  for n, (src, dst, rows) in enumerate(GATE_SRC)]
        for cp in copies:
            cp.start()
        for cp in copies:
            cp.wait()
        wg[...] = stage[...].astype(BF16)

    gt = _dot_nt(wg[...], h_ref[...])
    gt_ref[...] = gt
    g_ref[...] = gt.T


def _gates(h, wt, layer):
    t, k = h.shape
    tm = _tile(t, 512)
    return pl.pallas_call(
        functools.partial(_gates_kernel, layer=layer),
        grid=(t // tm,),
        in_specs=[pl.BlockSpec((tm, k), lambda i: (i, 0)), pl.BlockSpec(memory_space=pl.ANY)],
        out_specs=[pl.BlockSpec((tm, LANE), lambda i: (i, 0)), pl.BlockSpec((LANE, tm), lambda i: (0, i))],
        out_shape=[jax.ShapeDtypeStruct((t, LANE), F32), jax.ShapeDtypeStruct((LANE, t), F32)],
        scratch_shapes=[pltpu.VMEM((LANE, k), F32), pltpu.VMEM((LANE, k), BF16), pltpu.SemaphoreType.DMA((len(GATE_SRC),))],
        compiler_params=_cparams("arbitrary"),
        name="gates",
    )(h, wt)


def _cumsum_mats(n, blk):
    r = _iota2((n, n), 0)
    c = _iota2((n, n), 1)
    same = (r // blk) == (c // blk)
    lower = jnp.where(same & (c <= r), 1.0, 0.0).astype(F32)
    upper = jnp.where(same & (r <= c), 1.0, 0.0).astype(F32)
    return lower, upper


def _shift_rows(x, n):
    z = jnp.zeros((abs(n), x.shape[1]), x.dtype)
    if n > 0:
        return jnp.concatenate([z, x[:x.shape[0] - n, :]], axis=0)
    return jnp.concatenate([x[-n:, :], z], axis=0)


def _rms_head(x, w_row):
    return x * lax.rsqrt(jnp.mean(x * x, axis=-1, keepdims=True) + EPS) * w_row


def _mlstm_kernel(p_ref, g_ref, gt_ref, bcol_ref, brow_ref, nw_ref, o_ref, c_scr, m_scr, *, lb):
    nh, dk, dv = M_HEADS, M_DK, M_DV
    scale = dk ** -0.5

    @pl.when(pl.program_id(1) == 0)
    def _():
        c_scr[...] = jnp.zeros_like(c_scr)
        m_scr[...] = jnp.zeros_like(m_scr)

    gcol = g_ref[...] + bcol_ref[...]
    fcol = _log_sigmoid(gcol)
    grow = gt_ref[0:2 * SUBLANE, :] + brow_ref[0:2 * SUBLANE, :]
    frow = _log_sigmoid(grow)
    nch = lb // CHUNK
    lower, upper = _cumsum_mats(lb, CHUNK)
    r = _iota2((lb, lb), 0)
    c = _iota2((lb, lb), 1)
    same = (r // CHUNK) == (c // CHUNK)
    causal = same & (c <= r)
    ones_bd = jnp.where(same, 1.0, 0.0).astype(F32)
    bcum_col = _sum01(lower, fcol)
    btot_col = _sum01(ones_bd, fcol)
    bcum_row = _sum01_t(frow, upper)
    one_col = jnp.where(_iota2((lb, LANE), 1) == 0, 1.0, 0.0).astype(F32)

    cols = []
    for h in range(nh):
        b_c = bcum_col[:, LN_MF + h:LN_MF + h + 1]
        bt_c = btot_col[:, LN_MF + h:LN_MF + h + 1]
        log_w = bt_c - b_c + gcol[:, LN_MI + h:LN_MI + h + 1]
        m = m_scr[h:h + 1, 0:1]
        mp_rows, mn_rows = [], []
        for ch in range(nch):
            r0 = ch * CHUNK
            mp_rows.append(jnp.broadcast_to(m, (CHUNK, 1)))
            m = jnp.maximum(bt_c[r0:r0 + 1, :] + m, jnp.max(log_w[r0:r0 + CHUNK, :], axis=0, keepdims=True))
            mn_rows.append(jnp.broadcast_to(m, (CHUNK, 1)))
        m_scr[h:h + 1, :] = jnp.broadcast_to(m, (1, LANE))
        cols.append((b_c, bt_c, log_w, jnp.concatenate(mp_rows, axis=0), jnp.concatenate(mn_rows, axis=0)))

    ss, w_inters, ems = [], [], []
    for h in range(nh):
        q = p_ref[:, h * dk:(h + 1) * dk]
        k = p_ref[:, nh * dk + h * dk:nh * dk + (h + 1) * dk]
        b_c, bt_c, log_w, mp_col, mn_col = cols[h]
        b_r = bcum_row[LN_MF + h:LN_MF + h + 1, :]
        i_r = grow[LN_MI + h:LN_MI + h + 1, :]
        log_d = jnp.where(causal, b_c - b_r + i_r, -jnp.inf)
        m_inter = b_c + mp_col
        m_t = jnp.maximum(m_inter, jnp.max(log_d, axis=-1, keepdims=True))
        ss.append((_dot_nt(q, k) * (scale * jnp.exp(log_d - m_t))).astype(BF16))
        w_inters.append(jnp.exp(m_inter - m_t) * scale)
        ems.append(jnp.exp(-m_t))

    intras, xs, cdecs = [], [], []
    for h in range(nh):
        v = p_ref[:, 2 * nh * dk + h * dv:2 * nh * dk + (h + 1) * dv]
        b_c, bt_c, log_w, mp_col, mn_col = cols[h]
        v_ext = jnp.concatenate([v.astype(F32), one_col], axis=1)
        intras.append(_dot(ss[h], v_ext.astype(BF16)))
        xs.append((jnp.exp(log_w - mn_col) * v_ext).astype(BF16))
        cdecs.append(jnp.exp(bt_c + mp_col - mn_col))

    inters = [[] for _ in range(nh)]
    for ch in range(nch):
        r0 = ch * CHUNK
        for h in range(nh):
            c_state = c_scr[h]
            inters[h].append(_dot(p_ref[r0:r0 + CHUNK, h * dk:(h + 1) * dk], c_state.astype(BF16)))
            k_c = p_ref[r0:r0 + CHUNK, nh * dk + h * dk:nh * dk + (h + 1) * dk]
            c_scr[h] = cdecs[h][r0:r0 + 1, :] * c_state + _dot_tn(k_c, xs[h][r0:r0 + CHUNK, :])

    for h in range(nh):
        og = p_ref[:, 2 * nh * dk + nh * dv + h * dv:2 * nh * dk + nh * dv + (h + 1) * dv]
        tot = w_inters[h] * jnp.concatenate(inters[h], axis=0) + intras[h]
        hh = tot[:, :dv] / jnp.maximum(jnp.abs(tot[:, dv:dv + 1]), ems[h])
        y = _sigmoid(og.astype(F32)) * _rms_head(hh, nw_ref[h:h + 1, :])
        o_ref[:, h * dv:(h + 1) * dv] = y.astype(o_ref.dtype)


def _mlstm(proj, gates, gates_t, gate_b, norm_w, bsz, seq):
    t = bsz * seq
    lb = _tile(seq, MIXER_ROWS)
    nb = seq // lb
    bias = jnp.zeros((LANE,), F32).at[LN_MI:LN_MI + 2 * M_HEADS].set(gate_b)
    return pl.pallas_call(
        functools.partial(_mlstm_kernel, lb=lb),
        grid=(bsz, nb),
        in_specs=[
            pl.BlockSpec((lb, M_W), lambda b, c: (b * nb + c, 0)),
            pl.BlockSpec((lb, LANE), lambda b, c: (b * nb + c, 0)),
            pl.BlockSpec((LANE, lb), lambda b, c: (0, b * nb + c)),
            pl.BlockSpec((1, LANE), lambda b, c: (0, 0)),
            pl.BlockSpec((LANE, 1), lambda b, c: (0, 0)),
            pl.BlockSpec((M_HEADS, M_DV), lambda b, c: (0, 0)),
        ],
        out_specs=pl.BlockSpec((lb, BRANCH_W), lambda b, c: (b * nb + c, 0)),
        out_shape=jax.ShapeDtypeStruct((t, BRANCH_W), BF16),
        scratch_shapes=[pltpu.VMEM((M_HEADS, M_DK, M_DV + LANE), F32), pltpu.VMEM((SUBLANE, LANE), F32)],
        compiler_params=_cparams("parallel", "arbitrary"),
        name="mlstm",
    )(proj, gates, gates_t, bias.reshape(1, LANE), bias.reshape(LANE, 1), norm_w)


def _unit_lower_inverses(a_list, blk_diag):
    n = a_list[0].shape[0]
    r = _iota2((n, n), 0)
    c = _iota2((n, n), 1)
    eye = jnp.where(r == c, 1.0, 0.0).astype(F32)
    base = SUBLANE
    in_base = (r // base) == (c // base)
    nd = [jnp.where(in_base, -a, 0.0).astype(BF16) for a in a_list]
    n2 = [_dot(m, m).astype(BF16) for m in nd]
    xs = [eye + m.astype(F32) for m in nd]
    xs = [x + _dot(x.astype(BF16), m) for x, m in zip(xs, n2)]
    n4 = [_dot(m, m).astype(BF16) for m in n2]
    xs = [x + _dot(x.astype(BF16), m) for x, m in zip(xs, n4)]
    blk = base
    while blk < blk_diag:
        sel = ((r // (2 * blk)) == (c // (2 * blk))) & ((r // blk) != (c // blk))
        offs = [jnp.where(sel, a, 0.0).astype(BF16) for a in a_list]
        xb = [x.astype(BF16) for x in xs]
        xa = [_dot(x, o).astype(BF16) for x, o in zip(xb, offs)]
        xs = [x - _dot(m, x_b) for x, m, x_b in zip(xs, xa, xb)]
        blk *= 2
    return xs


def _gdn_kernel(p_ref, z_ref, g_ref, gt_ref, pcol_ref, prow_ref, nw_ref, o_ref,
                s_scr, wq_scr, u_scr, kd_scr, pm_scr, qs_scr, *, lb):
    nh, dk, dv = DN_HEADS, DN_DK, DN_DV
    scale = dk ** -0.5

    @pl.when(pl.program_id(1) == 0)
    def _():
        s_scr[...] = jnp.zeros_like(s_scr)

    gc = g_ref[...]
    beta_col = _sigmoid(gc)
    gl_col = -jnp.exp(pcol_ref[0:1, :]) * _softplus(gc + pcol_ref[1:2, :])
    row0 = LN_DA % SUBLANE
    rbase = LN_DA - row0
    gr = gt_ref[rbase:rbase + 2 * SUBLANE, :]
    pr = prow_ref[rbase:rbase + 2 * SUBLANE, :]
    gl_row = -jnp.exp(pr[:, 0:1]) * _softplus(gr + pr[:, 1:2])

    lower, upper = _cumsum_mats(lb, CHUNK)
    r = _iota2((lb, lb), 0)
    c = _iota2((lb, lb), 1)
    same = (r // CHUNK) == (c // CHUNK)
    incl = same & (c <= r)
    strict = same & (c < r)
    ones_bd = jnp.where(same, 1.0, 0.0).astype(F32)
    gcum_col = _sum01(lower, gl_col)
    gtot_col = _sum01(ones_bd, gl_col)
    gcum_row = _sum01_t(gl_row, upper)

    ks, qs, vs, a_list, gccs, bcs = [], [], [], [], [], []
    for h in range(nh):
        qa = p_ref[:, h * dk:(h + 1) * dk]
        ka = p_ref[:, nh * dk + h * dk:nh * dk + (h + 1) * dk]
        va = p_ref[:, 2 * nh * dk + h * dv:2 * nh * dk + (h + 1) * dv]
        gcc = gcum_col[:, LN_DA + h:LN_DA + h + 1]
        gcr = gcum_row[row0 + h:row0 + h + 1, :]
        b_c = beta_col[:, LN_DB + h:LN_DB + h + 1]
        decay = jnp.exp(jnp.where(incl, gcc - gcr, -jnp.inf))
        kq = _dot_nt(jnp.concatenate([(ka.astype(F32) * b_c).astype(BF16), qa], axis=0), ka)
        a_list.append(kq[:lb, :] * jnp.where(strict, decay, 0.0))
        pm_scr[h] = (kq[lb:, :] * (scale * decay)).astype(BF16)
        ks.append(ka.astype(F32)); qs.append(qa.astype(F32)); vs.append(va.astype(F32)); gccs.append(gcc); bcs.append(b_c)
    xs = _unit_lower_inverses(a_list, CHUNK)
    for h in range(nh):
        eg = jnp.exp(gccs[h])
        rhs = jnp.concatenate([ks[h] * (bcs[h] * eg), vs[h] * bcs[h]], axis=1).astype(BF16)
        wu = _dot(xs[h].astype(BF16), rhs)
        w = wu[:, :dk].astype(BF16)
        qg = (qs[h] * (eg * scale)).astype(BF16)
        for ch in range(lb // CHUNK):
            wq_scr[h, 2 * ch * CHUNK:(2 * ch + 1) * CHUNK, :] = w[ch * CHUNK:(ch + 1) * CHUNK, :]
            wq_scr[h, (2 * ch + 1) * CHUNK:(2 * ch + 2) * CHUNK, :] = qg[ch * CHUNK:(ch + 1) * CHUNK, :]
        u_scr[h] = wu[:, dk:]
        gtot = gtot_col[:, LN_DA + h:LN_DA + h + 1]
        kd_scr[h] = (ks[h] * jnp.exp(gtot - gccs[h])).astype(BF16)

    for ch in range(lb // CHUNK):
        r0 = ch * CHUNK
        dec_row = jnp.exp(gtot_col[r0:r0 + 1, :])
        for h in range(nh):
            s_prev = s_scr[h]
            sb = s_prev.astype(BF16)
            ws = _dot(wq_scr[h, 2 * r0:2 * r0 + 2 * CHUNK, :], sb)
            u_new = u_scr[h, r0:r0 + CHUNK, :] - ws[:CHUNK, :]
            qs_scr[h, r0:r0 + CHUNK, :] = ws[CHUNK:, :]
            u_scr[h, r0:r0 + CHUNK, :] = u_new
            s_scr[h] = dec_row[:, LN_DA + h:LN_DA + h + 1] * s_prev + _dot_tn(kd_scr[h, r0:r0 + CHUNK, :], u_new.astype(BF16))

    for h in range(nh):
        o = qs_scr[h] + _dot(pm_scr[h], u_scr[h].astype(BF16))
        z = z_ref[:, h * dv:(h + 1) * dv].astype(F32)
        o_ref[:, h * dv:(h + 1) * dv] = (_rms_head(o, nw_ref[...]) * _silu(z)).astype(o_ref.dtype)


def _gdn(qkv, z, gates, gates_t, a_log, dt_bias, norm_w, bsz, seq):
    t = bsz * seq
    lb = _tile(seq, GDN_ROWS)
    nb = seq // lb
    nh = DN_HEADS
    par = jnp.zeros((2, LANE), F32)
    par = par.at[0, LN_DA:LN_DA + nh].set(a_log).at[1, LN_DA:LN_DA + nh].set(dt_bias)
    return pl.pallas_call(
        functools.partial(_gdn_kernel, lb=lb),
        grid=(bsz, nb),
        in_specs=[
            pl.BlockSpec((lb, DQKV_W), lambda b, c: (b * nb + c, 0)),
            pl.BlockSpec((lb, DZ_W), lambda b, c: (b * nb + c, 0)),
            pl.BlockSpec((lb, LANE), lambda b, c: (b * nb + c, 0)),
            pl.BlockSpec((LANE, lb), lambda b, c: (0, b * nb + c)),
            pl.BlockSpec((2, LANE), lambda b, c: (0, 0)),
            pl.BlockSpec((LANE, 2), lambda b, c: (0, 0)),
            pl.BlockSpec((1, DN_DV), lambda b, c: (0, 0)),
        ],
        out_specs=pl.BlockSpec((lb, BRANCH_W), lambda b, c: (b * nb + c, 0)),
        out_shape=jax.ShapeDtypeStruct((t, BRANCH_W), BF16),
        scratch_shapes=[
            pltpu.VMEM((nh, DN_DK, DN_DV), F32),
            pltpu.VMEM((nh, 2 * lb, DN_DK), BF16),
            pltpu.VMEM((nh, lb, DN_DV), F32),
            pltpu.VMEM((nh, lb, DN_DK), BF16),
            pltpu.VMEM((nh, lb, lb), BF16),
            pltpu.VMEM((nh, lb, DN_DV), F32),
        ],
        compiler_params=_cparams("parallel", "arbitrary"),
        name="gdn",
    )(qkv, z, gates, gates_t, par, par.T, norm_w.reshape(1, DN_DV))


def _gla_kernel(p_ref, g_ref, wa_ref, ba_ref, nw_ref, o_ref, s_scr, *, lb):
    nh, dk, dv = G_HEADS, G_DK, G_DV
    scale = dk ** -0.5
    grp = GLA_GROUP
    nsub = grp // G_CHUNK

    @pl.when(pl.program_id(1) == 0)
    def _():
        s_scr[...] = jnp.zeros_like(s_scr)

    log_a = _log_sigmoid(_dot(g_ref[...].astype(BF16), wa_ref[...].astype(BF16)) + ba_ref[...]) * (1.0 / G_TAU)
    lower, _ = _cumsum_mats(grp, G_CHUNK)
    r = _iota2((grp, grp), 0)
    c = _iota2((grp, grp), 1)
    dblk = (r // G_CHUNK) - (c // G_CHUNK)
    ones_blk = jnp.where(dblk == 0, 1.0, 0.0).astype(F32)
    diag_lower = (dblk == 0) & (c <= r)

    for gi in range(lb // grp):
        r0 = gi * grp
        la = log_a[r0:r0 + grp, :]
        lc_all = _sum01(lower, la)
        le_all = _sum01(ones_blk, la)
        before = [_shift_rows(le_all, d * G_CHUNK) for d in range(1, nsub)]
        after = [_shift_rows(le_all, -d * G_CHUNK) for d in range(1, nsub)]
        skipped = [None, None]
        for d in range(2, nsub):
            skipped.append(before[d - 2] if skipped[d - 1] is None else skipped[d - 1] + before[d - 2])
        to_start = functools.reduce(lambda a, b: a + b, before)
        to_end = functools.reduce(lambda a, b: a + b, after)
        total = le_all[0:1, :] + to_end[0:1, :]

        ps, q0s, kends = [], [], []
        for h in range(nh):
            sl = slice(h * dk, (h + 1) * dk)
            q = p_ref[r0:r0 + grp, h * dk:(h + 1) * dk].astype(F32)
            k = p_ref[r0:r0 + grp, nh * dk + h * dk:nh * dk + (h + 1) * dk].astype(F32)
            lc = lc_all[:, sl]
            q_in = q * (jnp.exp(lc) * scale)
            q_b = q_in.astype(BF16)
            k_in = (k * jnp.exp(-lc)).astype(BF16)
            kd = k * jnp.exp(le_all[:, sl] - lc)
            kd_b = kd.astype(BF16)
            p = jnp.where(diag_lower, _dot_nt(q_b, k_in), 0.0)
            p = p + jnp.where(dblk == 1, _dot_nt(q_b, kd_b), 0.0)
            for d in range(2, nsub):
                qd = (q_in * jnp.exp(skipped[d][:, sl])).astype(BF16)
                p = p + jnp.where(dblk == d, _dot_nt(qd, kd_b), 0.0)
            ps.append(p.astype(BF16))
            q0s.append((q_in * jnp.exp(to_start[:, sl])).astype(BF16))
            kends.append((kd * jnp.exp(to_end[:, sl])).astype(BF16))
        outs = []
        for h in range(nh):
            v = p_ref[r0:r0 + grp, 2 * nh * dk + h * dv:2 * nh * dk + (h + 1) * dv]
            st = s_scr[h]
            outs.append(_dot(ps[h], v) + _dot_nt(q0s[h], st.astype(BF16)))
            s_scr[h] = jnp.exp(total[:, h * dk:(h + 1) * dk]) * st + _dot_tn(v, kends[h])
        for h in range(nh):
            rg = p_ref[r0:r0 + grp, 2 * nh * dk + nh * dv + h * dv:2 * nh * dk + nh * dv + (h + 1) * dv]
            y = _rms_head(outs[h], nw_ref[...]) * _silu(rg.astype(F32))
            o_ref[r0:r0 + grp, h * dv:(h + 1) * dv] = y.astype(o_ref.dtype)


def _gla(proj, gates, w_a2, b_a, norm_w, bsz, seq):
    t = bsz * seq
    lb = _tile(seq, GLA_ROWS)
    nb = seq // lb
    gw = G_HEADS * G_DK
    wa = jnp.zeros((LANE, gw), F32).at[LN_GLR:LN_GLR + G_RANK, :].set(w_a2)
    return pl.pallas_call(
        functools.partial(_gla_kernel, lb=lb),
        grid=(bsz, nb),
        in_specs=[
            pl.BlockSpec((lb, G_W), lambda b, c: (b * nb + c, 0)),
            pl.BlockSpec((lb, LANE), lambda b, c: (b * nb + c, 0)),
            pl.BlockSpec((LANE, gw), lambda b, c: (0, 0)),
            pl.BlockSpec((1, gw), lambda b, c: (0, 0)),
            pl.BlockSpec((1, G_DV), lambda b, c: (0, 0)),
        ],
        out_specs=pl.BlockSpec((lb, BRANCH_W), lambda b, c: (b * nb + c, 0)),
        out_shape=jax.ShapeDtypeStruct((t, BRANCH_W), BF16),
        scratch_shapes=[pltpu.VMEM((G_HEADS, G_DV, G_DK), F32)],
        compiler_params=_cparams("parallel", "arbitrary"),
        name="gla",
    )(proj, gates, wa, b_a.reshape(1, gw), norm_w.reshape(1, G_DV))


def _merge_kernel(ym_ref, yd_ref, yg_ref, gm_ref, gd_ref, gg_ref, w_hbm, o_ref, stage, wb, sem, *, tn, layer, nj):
    tm = o_ref.shape[0]

    @pl.when(pl.program_id(1) == 0)
    def _():
        for g in range(N_BRANCH):
            _load_weight_tile(w_hbm.at[layer], stage.at[g], wb.at[g], sem.at[g], layer=g, col0=0, tn=tn, nj=nj, k_minor=False)

    y_refs = (ym_ref, yd_ref, yg_ref)
    g_refs = (gm_ref, gd_ref, gg_ref)
    gw = _group_width(tn)
    tr = max(tm // MERGE_ROW_SPLIT, SUBLANE)

    def products(r0):
        return [[_dot(y_refs[g][r0:r0 + tr, :], wb[g, :, g0:g0 + gw]) for g in range(N_BRANCH)] for g0 in range(0, tn, gw)]

    def epilogue(r0, raw):
        for n, g0 in enumerate(range(0, tn, gw)):
            acc = _sigmoid(g_refs[0][r0:r0 + tr, g0:g0 + gw].astype(F32)) * raw[n][0]
            for g in range(1, N_BRANCH):
                acc = acc + _sigmoid(g_refs[g][r0:r0 + tr, g0:g0 + gw].astype(F32)) * raw[n][g]
            o_ref[r0:r0 + tr, g0:g0 + gw] = acc.astype(o_ref.dtype)

    pending = None
    for r0 in range(0, tm, tr):
        raw = products(r0)
        if pending is not None:
            epilogue(*pending)
        pending = (r0, raw)
    epilogue(*pending)


def _merge(y_m, y_d, y_g, w_branch, layer, merge_pre, d_model):
    t = y_m.shape[0]
    tm = _tile(t, ROW_TILE)
    tn = _tile(d_model, COL_TILE_FFN)
    nj = d_model // tn
    yspec = pl.BlockSpec((tm, BRANCH_W), lambda j, i: (i, 0))
    gspec = lambda g: pl.BlockSpec((tm, tn), lambda j, i: (i, g * nj + j))
    return pl.pallas_call(
        functools.partial(_merge_kernel, tn=tn, layer=layer, nj=nj),
        grid=(nj, t // tm),
        in_specs=[yspec, yspec, yspec, gspec(0), gspec(1), gspec(2), pl.BlockSpec(memory_space=pl.ANY)],
        out_specs=pl.BlockSpec((tm, tn), lambda j, i: (i, j)),
        out_shape=jax.ShapeDtypeStruct((t, d_model), BF16),
        scratch_shapes=[pltpu.VMEM((N_BRANCH, BRANCH_W, tn), F32), pltpu.VMEM((N_BRANCH, BRANCH_W, tn), BF16),
                        pltpu.SemaphoreType.DMA((N_BRANCH,))],
        compiler_params=_cparams("arbitrary", "arbitrary"),
        name="merge",
    )(y_m, y_d, y_g, merge_pre, merge_pre, merge_pre, w_branch)


def _ffn_up_kernel(h_ref, w_hbm, cg_ref, cu_ref, o_ref, stage, wb, sem, tail_g, tail_u, *, tn, d_ff, rows_per_seq, layer, nj):
    i = pl.program_id(1)
    tm = h_ref.shape[0]

    @pl.when(i == 0)
    def _():
        for m, col0 in enumerate((0, d_ff)):
            _load_weight_tile(w_hbm, stage.at[m], wb.at[m], sem.at[m], layer=layer, col0=col0, tn=tn, nj=nj, k_minor=False)

    @pl.when((i * tm) % rows_per_seq == 0)
    def _():
        tail_g[...] = jnp.zeros_like(tail_g)
        tail_u[...] = jnp.zeros_like(tail_u)

    gw = _group_width(tn)
    tr = max(tm // FFN_ROW_SPLIT, SUBLANE)

    def products(r0):
        h = h_ref[r0:r0 + tr, :]
        return [(_dot(h, wb[0, :, g0:g0 + gw]), _dot(h, wb[1, :, g0:g0 + gw])) for g0 in range(0, tn, gw)]

    def epilogue(r0, raw):
        for n, g0 in enumerate(range(0, tn, gw)):
            gate = _causal_conv(raw[n][0], tail_g, g0, cg_ref, FFN_CONV)
            up = _causal_conv(raw[n][1], tail_u, g0, cu_ref, FFN_CONV)
            o_ref[r0:r0 + tr, g0:g0 + gw] = (_silu(gate) * up).astype(o_ref.dtype)

    pending = None
    for r0 in range(0, tm, tr):
        raw = products(r0)
        if pending is not None:
            epilogue(*pending)
        pending = (r0, raw)
    epilogue(*pending)


def _ffn_up(h, w_up, conv_w, layer, seq):
    t, k = h.shape
    d_ff = w_up.shape[-1] // 2
    tm = _tile(seq, ROW_TILE)
    tn = _tile(d_ff, COL_TILE_FFN)
    nj = d_ff // tn
    return pl.pallas_call(
        functools.partial(_ffn_up_kernel, tn=tn, d_ff=d_ff, rows_per_seq=seq, layer=layer, nj=nj),
        grid=(nj, t // tm),
        in_specs=[
            pl.BlockSpec((tm, k), lambda j, i: (i, 0)),
            pl.BlockSpec(memory_space=pl.ANY),
            pl.BlockSpec((None, FFN_CONV, tn), lambda j, i: (layer, 0, j)),
            pl.BlockSpec((None, FFN_CONV, tn), lambda j, i: (layer, 0, nj + j)),
        ],
        out_specs=pl.BlockSpec((tm, tn), lambda j, i: (i, j)),
        out_shape=jax.ShapeDtypeStruct((t, d_ff), BF16),
        scratch_shapes=[pltpu.VMEM((2, k, tn), F32), pltpu.VMEM((2, k, tn), BF16), pltpu.SemaphoreType.DMA((2,)),
                        pltpu.VMEM((SUBLANE, tn), F32), pltpu.VMEM((SUBLANE, tn), F32)],
        compiler_params=_cparams("arbitrary", "arbitrary"),
        name="ffn_up",
    )(h, w_up, conv_w, conv_w)


def kernel(x, norm_mix_pre, norm_mix_post, norm_ffn_pre, norm_ffn_post, w_in, mlstm_gate_b, mlstm_norm, dn_conv, dn_a_log, dn_dt_bias, dn_norm, gla_w_a2, gla_b_a, gla_norm, w_branch, w_out, w_ffn_up, ffn_conv, w_ffn_down):
    bsz, seq, d_model = x.shape
    depth = w_in.shape[0]
    t = bsz * seq
    x2 = x.reshape(t, d_model)
    wt = jnp.transpose(w_in, (2, 0, 1))
    proj = functools.partial(_wsmm, k_minor=True)
    dense = functools.partial(_wsmm, k_minor=False)
    h = _norm_cast(x2, norm_mix_pre[0])
    for l in range(depth):
        proj_m = proj(h, wt, l, OFF_M, M_W, BF16, tn=COL_TILE_SLAB, name="proj_m")
        d_qkv = proj(h, wt, l, OFF_D, DQKV_W, BF16, tn=COL_TILE_SLAB, name="proj_dqkv",
                     conv=dn_conv[l], seq=seq, l2_cols=2 * DN_HEADS * DN_DK)
        d_z = proj(h, wt, l, OFF_DZ, DZ_W, BF16, tn=COL_TILE_SLAB, name="proj_dz")
        proj_g = proj(h, wt, l, OFF_G, G_W, BF16, tn=COL_TILE_SLAB, name="proj_g")
        merge_pre = proj(h, wt, l, OFF_MERGE, N_BRANCH * d_model, BF16, tn=COL_TILE_WIDE, name="proj_merge")
        gates, gates_t = _gates(h, wt, l)
        y_m = _mlstm(proj_m, gates, gates_t, mlstm_gate_b[l], mlstm_norm[l], bsz, seq)
        y_d = _gdn(d_qkv, d_z, gates, gates_t, dn_a_log[l], dn_dt_bias[l], dn_norm[l], bsz, seq)
        y_g = _gla(proj_g, gates, gla_w_a2[l], gla_b_a[l], gla_norm[l], bsz, seq)
        merged = _merge(y_m, y_d, y_g, w_branch, l, merge_pre, d_model)
        mix = dense(merged, w_out, l, 0, d_model, BF16, tn=COL_TILE_WIDE, name="out_proj")
        x2, h = _resid_norm(x2, mix, norm_mix_post[l], norm_ffn_pre[l])
        act = _ffn_up(h, w_ffn_up, ffn_conv, l, seq)
        down = dense(act, w_ffn_down, l, 0, d_model, BF16, tn=COL_TILE_FFN, name="ffn_down")
        x2, h = _resid_norm(x2, down, norm_ffn_post[l], norm_mix_pre[l + 1] if l + 1 < depth else None)
    return x2.reshape(bsz, seq, d_model)
```

```python
import functools

import jax
import jax.numpy as jnp
from jax import lax
from jax.experimental import pallas as pl
from jax.experimental.pallas import tpu as pltpu

F32 = jnp.float32
BF16 = jnp.bfloat16
EPS = 1e-6
LANE = 128
SUBLANE = 8
MXU_WIDTH = 256
VMEM_LIMIT = 56 * 1024 * 1024

M_HEADS, M_DK, M_DV = 6, 128, 256
DN_HEADS, DN_DK, DN_DV, DN_CONV = 12, 128, 128, 4
G_HEADS, G_DK, G_DV, G_RANK, G_TAU = 6, 128, 256, 16, 16.0
CHUNK, G_CHUNK = 64, 16
N_BRANCH, BRANCH_W, FFN_CONV = 3, 1536, 3

M_W = 2 * M_HEADS * M_DK + 2 * M_HEADS * M_DV
DQKV_W = 2 * DN_HEADS * DN_DK + DN_HEADS * DN_DV
DZ_W = DN_HEADS * DN_DV
G_W = 2 * G_HEADS * G_DK + 2 * G_HEADS * G_DV
OFF_M = 0
OFF_MGATE = OFF_M + M_W
OFF_D = OFF_MGATE + 2 * M_HEADS
OFF_DZ = OFF_D + DQKV_W
OFF_DGATE = OFF_DZ + DZ_W
OFF_G = OFF_DGATE + 2 * DN_HEADS
OFF_GLR = OFF_G + G_W
OFF_MERGE = OFF_GLR + G_RANK
GATE_SRC = ((OFF_MGATE, 0, 2 * SUBLANE), (OFF_DGATE, 2 * SUBLANE, 2 * DN_HEADS), (OFF_GLR, 2 * SUBLANE + 2 * DN_HEADS, G_RANK))
LN_MI = 0
LN_MF = LN_MI + M_HEADS
LN_DB = GATE_SRC[1][1]
LN_DA = LN_DB + DN_HEADS
LN_GLR = GATE_SRC[2][1]
assert 2 * M_HEADS <= GATE_SRC[0][2] and LN_GLR + G_RANK <= LANE

ROW_TILE = 1024
COL_TILE_SLAB = 768
COL_TILE_WIDE = 1024
COL_TILE_FFN = 512
EPILOGUE_HALVINGS = 1
FFN_HALVINGS = 1
MERGE_HALVINGS = 2
MIXER_ROWS = 256
GDN_ROWS = 128
GLA_ROWS = 256
GLA_GROUP = 64


def _cparams(*sem):
    return pltpu.CompilerParams(dimension_semantics=sem, vmem_limit_bytes=VMEM_LIMIT)


def _dot(a, b):
    return jnp.dot(a, b, preferred_element_type=F32)


def _dot_nt(a, b):
    return lax.dot_general(a, b, (((1,), (1,)), ((), ())), preferred_element_type=F32)


def _dot_tn(a, b):
    return lax.dot_general(a, b, (((0,), (0,)), ((), ())), preferred_element_type=F32)


def _split3(x):
    hi = x.astype(BF16)
    r1 = x - hi.astype(F32)
    mid = r1.astype(BF16)
    lo = (r1 - mid.astype(F32)).astype(BF16)
    return hi, mid, lo


def _sum01(m01, x):
    m = m01.astype(BF16)
    hi, mid, lo = _split3(x)
    return _dot(m, hi) + _dot(m, mid) + _dot(m, lo)


def _sum01_t(x, m01):
    m = m01.astype(BF16)
    hi, mid, lo = _split3(x)
    return _dot(hi, m) + _dot(mid, m) + _dot(lo, m)


def _sigmoid(x):
    return 1.0 / (1.0 + jnp.exp(-x))


def _silu(x):
    return x * _sigmoid(x)


def _log_sigmoid(x):
    return jnp.minimum(x, 0.0) - jnp.log(1.0 + jnp.exp(-jnp.abs(x)))


def _softplus(x):
    return jnp.maximum(x, 0.0) + jnp.log(1.0 + jnp.exp(-jnp.abs(x)))


def _iota2(shape, axis):
    return lax.broadcasted_iota(jnp.int32, shape, axis)


def _tile(n, pref):
    t = min(n, pref)
    assert n % t == 0, (n, pref)
    return t


def _group_width(tn):
    return MXU_WIDTH if tn % MXU_WIDTH == 0 else tn


def _norm_cast_kernel(x_ref, g_ref, o_ref):
    x = x_ref[...]
    y = x * lax.rsqrt(jnp.mean(x * x, axis=-1, keepdims=True) + EPS)
    o_ref[...] = (y * g_ref[...]).astype(o_ref.dtype)


def _norm_cast(x2, g):
    t, d = x2.shape
    tr = _tile(t, 256)
    return pl.pallas_call(
        _norm_cast_kernel,
        grid=(t // tr,),
        in_specs=[pl.BlockSpec((tr, d), lambda i: (i, 0)), pl.BlockSpec((1, d), lambda i: (0, 0))],
        out_specs=pl.BlockSpec((tr, d), lambda i: (i, 0)),
        out_shape=jax.ShapeDtypeStruct((t, d), BF16),
        compiler_params=_cparams("parallel"),
        name="norm_cast",
    )(x2, g.reshape(1, d))


def _resid_norm_kernel(x_ref, y_ref, gp_ref, gn_ref, xo_ref, ho_ref):
    y = y_ref[...].astype(F32)
    yn = y * lax.rsqrt(jnp.mean(y * y, axis=-1, keepdims=True) + EPS) * gp_ref[...]
    xn = x_ref[...] + yn
    xo_ref[...] = xn
    hn = xn * lax.rsqrt(jnp.mean(xn * xn, axis=-1, keepdims=True) + EPS)
    ho_ref[...] = (hn * gn_ref[...]).astype(ho_ref.dtype)


def _resid_kernel(x_ref, y_ref, gp_ref, xo_ref):
    y = y_ref[...].astype(F32)
    yn = y * lax.rsqrt(jnp.mean(y * y, axis=-1, keepdims=True) + EPS) * gp_ref[...]
    xo_ref[...] = x_ref[...] + yn


def _resid_norm(x2, y, g_post, g_next):
    t, d = x2.shape
    tr = _tile(t, 256)
    row = pl.BlockSpec((tr, d), lambda i: (i, 0))
    vec = pl.BlockSpec((1, d), lambda i: (0, 0))
    if g_next is None:
        return pl.pallas_call(
            _resid_kernel,
            grid=(t // tr,),
            in_specs=[row, row, vec],
            out_specs=row,
            out_shape=jax.ShapeDtypeStruct((t, d), F32),
            compiler_params=_cparams("parallel"),
            name="resid",
        )(x2, y, g_post.reshape(1, d)), None
    return pl.pallas_call(
        _resid_norm_kernel,
        grid=(t // tr,),
        in_specs=[row, row, vec, vec],
        out_specs=[row, row],
        out_shape=[jax.ShapeDtypeStruct((t, d), F32), jax.ShapeDtypeStruct((t, d), BF16)],
        compiler_params=_cparams("parallel"),
        name="resid_norm",
    )(x2, y, g_post.reshape(1, d), g_next.reshape(1, d))


def _weight_window(w_hbm, layer, col, tn, k_minor):
    if k_minor:
        return w_hbm.at[pl.ds(col, tn), layer, :]
    return w_hbm.at[layer, :, pl.ds(pl.multiple_of(col, LANE), tn)]


def _load_weight_tile(w_hbm, stage, wb, sem, *, layer, col0, tn, nj, k_minor):
    j = pl.program_id(0)

    def copy(jj):
        return pltpu.make_async_copy(_weight_window(w_hbm, layer, col0 + jj * tn, tn, k_minor), stage, sem)

    @pl.when(j == 0)
    def _():
        copy(0).start()

    copy(j).wait()
    wb[...] = stage[...].astype(BF16)

    @pl.when(j + 1 < nj)
    def _():
        copy(j + 1).start()


def _tile_dot(h, wb, g0, gw, k_minor):
    if k_minor:
        return _dot_nt(h, wb[g0:g0 + gw, :])
    return _dot(h, wb[:, g0:g0 + gw])


def _wsmm_kernel(h_ref, w_hbm, o_ref, stage, wb, sem, *, tn, k_minor, **tile):
    @pl.when(pl.program_id(1) == 0)
    def _():
        _load_weight_tile(w_hbm, stage, wb, sem, tn=tn, k_minor=k_minor, **tile)

    h = h_ref[...]
    gw = _group_width(tn)
    for g0 in range(0, tn, gw):
        o_ref[:, g0:g0 + gw] = _tile_dot(h, wb, g0, gw, k_minor).astype(o_ref.dtype)


def _causal_conv(x, tail_ref, g0, w_ref, taps):
    tm, gw = x.shape
    xp = jnp.concatenate([tail_ref[:, g0:g0 + gw], x], axis=0)
    tail_ref[:, g0:g0 + gw] = x[tm - SUBLANE:, :]
    acc = w_ref[taps - 1:taps, g0:g0 + gw] * x
    for d in range(1, taps):
        acc = acc + w_ref[taps - 1 - d:taps - d, g0:g0 + gw] * pltpu.roll(xp, d, 0)[SUBLANE:, :]
    return acc


def _row_pieces(tm, halvings):
    sizes = [tm >> (n + 1) for n in range(halvings)]
    sizes.append(tm - sum(sizes))
    assert all(s % SUBLANE == 0 and s > 0 for s in sizes), sizes
    starts = [sum(sizes[:n]) for n in range(len(sizes))]
    return list(zip(starts, sizes))


def _pipelined_pieces(tm, halvings, products, epilogue):
    pending = None
    for r0, tr in _row_pieces(tm, halvings):
        raw = products(r0, tr)
        if pending is not None:
            epilogue(*pending)
        pending = (r0, tr, raw)
    epilogue(*pending)


def _wsmm_conv_kernel(h_ref, w_hbm, cw_ref, o_ref, stage, wb, sem, tail, *, tn, k_minor, rows_per_seq, l2_tiles, **tile):
    j = pl.program_id(0)
    i = pl.program_id(1)
    tm = h_ref.shape[0]

    @pl.when(i == 0)
    def _():
        _load_weight_tile(w_hbm, stage, wb, sem, tn=tn, k_minor=k_minor, **tile)

    @pl.when((i * tm) % rows_per_seq == 0)
    def _():
        tail[...] = jnp.zeros_like(tail)

    gw = _group_width(tn)

    def products(r0, tr):
        h = h_ref[r0:r0 + tr, :]
        return [_tile_dot(h, wb, g0, gw, k_minor) for g0 in range(0, tn, gw)]

    def epilogue(r0, tr, raw):
        for n, g0 in enumerate(range(0, tn, gw)):
            y = _silu(_causal_conv(raw[n], tail, g0, cw_ref, DN_CONV))
            for s0 in range(0, gw, DN_DK):
                ys = y[:, s0:s0 + DN_DK]
                yn = ys * lax.rsqrt(jnp.sum(ys * ys, axis=-1, keepdims=True) + EPS)
                o_ref[r0:r0 + tr, g0 + s0:g0 + s0 + DN_DK] = jnp.where(j < l2_tiles, yn, ys).astype(o_ref.dtype)

    _pipelined_pieces(tm, EPILOGUE_HALVINGS, products, epilogue)


def _wsmm(h, w, layer, col0, n, out_dtype, *, tn, k_minor, name, conv=None, seq=None, l2_cols=0):
    t, k = h.shape
    tm = _tile(t, ROW_TILE if seq is None else min(ROW_TILE, seq))
    tn = _tile(n, tn)
    nj = n // tn
    tile = dict(layer=layer, col0=col0, nj=nj)
    wshape = (tn, k) if k_minor else (k, tn)
    in_specs = [pl.BlockSpec((tm, k), lambda j, i: (i, 0)), pl.BlockSpec(memory_space=pl.ANY)]
    scratch = [pltpu.VMEM(wshape, F32), pltpu.VMEM(wshape, BF16), pltpu.SemaphoreType.DMA(())]
    args = [h, w]
    if conv is None:
        body = functools.partial(_wsmm_kernel, tn=tn, k_minor=k_minor, **tile)
    else:
        assert seq % tm == 0 and l2_cols % tn == 0
        body = functools.partial(_wsmm_conv_kernel, tn=tn, k_minor=k_minor, rows_per_seq=seq, l2_tiles=l2_cols // tn, **tile)
        in_specs.append(pl.BlockSpec((conv.shape[0], tn), lambda j, i: (0, j)))
        scratch.append(pltpu.VMEM((SUBLANE, tn), F32))
        args.append(conv)
    return pl.pallas_call(
        body,
        grid=(nj, t // tm),
        in_specs=in_specs,
        out_specs=pl.BlockSpec((tm, tn), lambda j, i: (i, j)),
        out_shape=jax.ShapeDtypeStruct((t, n), out_dtype),
        scratch_shapes=scratch,
        compiler_params=_cparams("arbitrary", "arbitrary"),
        name=name,
    )(*args)


def _gates_kernel(h_ref, wt_hbm, g_ref, gt_ref, stage, wg, sem, *, layer):
    @pl.when(pl.program_id(0) == 0)
    def _():
        stage[...] = jnp.zeros_like(stage)
        copies = [pltpu.make_async_copy(wt_hbm.at[pl.ds(src, rows), layer, :], stage.at[pl.ds(dst, rows), :], sem.at[n])
                  for n, (src, dst, rows) in enumerate(GATE_SRC)]
        for cp in copies:
            cp.start()
        for cp in copies:
            cp.wait()
        wg[...] = stage[...].astype(BF16)

    gt = _dot_nt(wg[...], h_ref[...])
    gt_ref[...] = gt
    g_ref[...] = gt.T


def _gates(h, wt, layer):
    t, k = h.shape
    tm = _tile(t, 512)
    return pl.pallas_call(
        functools.partial(_gates_kernel, layer=layer),
        grid=(t // tm,),
        in_specs=[pl.BlockSpec((tm, k), lambda i: (i, 0)), pl.BlockSpec(memory_space=pl.ANY)],
        out_specs=[pl.BlockSpec((tm, LANE), lambda i: (i, 0)), pl.BlockSpec((LANE, tm), lambda i: (0, i))],
        out_shape=[jax.ShapeDtypeStruct((t, LANE), F32), jax.ShapeDtypeStruct((LANE, t), F32)],
        scratch_shapes=[pltpu.VMEM((LANE, k), F32), pltpu.VMEM((LANE, k), BF16), pltpu.SemaphoreType.DMA((len(GATE_SRC),))],
        compiler_params=_cparams("arbitrary"),
        name="gates",
    )(h, wt)


def _cumsum_mats(n, blk):
    r = _iota2((n, n), 0)
    c = _iota2((n, n), 1)
    same = (r // blk) == (c // blk)
    lower = jnp.where(same & (c <= r), 1.0, 0.0).astype(F32)
    upper = jnp.where(same & (r <= c), 1.0, 0.0).astype(F32)
    return lower, upper


def _shift_rows(x, n):
    z = jnp.zeros((abs(n), x.shape[1]), x.dtype)
    if n > 0:
        return jnp.concatenate([z, x[:x.shape[0] - n, :]], axis=0)
    return jnp.concatenate([x[-n:, :], z], axis=0)


def _rms_head(x, w_row):
    return x * lax.rsqrt(jnp.mean(x * x, axis=-1, keepdims=True) + EPS) * w_row


def _mlstm_kernel(p_ref, gt_ref, brow_ref, nw_ref, o_ref, c_scr, m_scr, *, lb):
    nh, dk, dv = M_HEADS, M_DK, M_DV
    scale = dk ** -0.5

    @pl.when(pl.program_id(1) == 0)
    def _():
        c_scr[...] = jnp.zeros_like(c_scr)
        m_scr[...] = jnp.zeros_like(m_scr)

    rows = 2 * SUBLANE
    grow = gt_ref[0:rows, :] + brow_ref[0:rows, :]
    frow = _log_sigmoid(grow)
    nch = lb // CHUNK
    _, upper = _cumsum_mats(lb, CHUNK)
    r = _iota2((lb, lb), 0)
    c = _iota2((lb, lb), 1)
    same = (r // CHUNK) == (c // CHUNK)
    causal = same & (c <= r)
    ones_bd = jnp.where(same, 1.0, 0.0).astype(F32)
    bcum_row = _sum01_t(frow, upper)
    btot_row = _sum01_t(frow, ones_bd)
    tcols = jnp.concatenate([bcum_row, btot_row, grow, jnp.zeros((LANE - 3 * rows, lb), F32)], axis=0).T
    bcum_col = tcols[:, 0:rows]
    btot_col = tcols[:, rows:2 * rows]
    gcol = tcols[:, 2 * rows:3 * rows]
    one_col = jnp.where(_iota2((lb, LANE), 1) == 0, 1.0, 0.0).astype(F32)

    cols = []
    for h in range(nh):
        b_c = bcum_col[:, LN_MF + h:LN_MF + h + 1]
        bt_c = btot_col[:, LN_MF + h:LN_MF + h + 1]
        log_w = bt_c - b_c + gcol[:, LN_MI + h:LN_MI + h + 1]
        m = m_scr[h:h + 1, 0:1]
        mp_rows, mn_rows = [], []
        for ch in range(nch):
            r0 = ch * CHUNK
            mp_rows.append(jnp.broadcast_to(m, (CHUNK, 1)))
            m = jnp.maximum(bt_c[r0:r0 + 1, :] + m, jnp.max(log_w[r0:r0 + CHUNK, :], axis=0, keepdims=True))
            mn_rows.append(jnp.broadcast_to(m, (CHUNK, 1)))
        m_scr[h:h + 1, :] = jnp.broadcast_to(m, (1, LANE))
        cols.append((b_c, bt_c, log_w, jnp.concatenate(mp_rows, axis=0), jnp.concatenate(mn_rows, axis=0)))

    ss, w_inters, ems = [], [], []
    for h in range(nh):
        q = p_ref[:, h * dk:(h + 1) * dk]
        k = p_ref[:, nh * dk + h * dk:nh * dk + (h + 1) * dk]
        b_c, bt_c, log_w, mp_col, mn_col = cols[h]
        b_r = bcum_row[LN_MF + h:LN_MF + h + 1, :]
        i_r = grow[LN_MI + h:LN_MI + h + 1, :]
        log_d = jnp.where(causal, b_c - b_r + i_r, -jnp.inf)
        m_inter = b_c + mp_col
        m_t = jnp.maximum(m_inter, jnp.max(log_d, axis=-1, keepdims=True))
        ss.append((_dot_nt(q, k) * (scale * jnp.exp(log_d - m_t))).astype(BF16))
        w_inters.append(jnp.exp(m_inter - m_t) * scale)
        ems.append(jnp.exp(-m_t))

    intras, xs, cdecs = [], [], []
    for h in range(nh):
        v = p_ref[:, 2 * nh * dk + h * dv:2 * nh * dk + (h + 1) * dv]
        b_c, bt_c, log_w, mp_col, mn_col = cols[h]
        v_ext = jnp.concatenate([v.astype(F32), one_col], axis=1)
        intras.append(_dot(ss[h], v_ext.astype(BF16)))
        xs.append((jnp.exp(log_w - mn_col) * v_ext).astype(BF16))
        cdecs.append(jnp.exp(bt_c + mp_col - mn_col))

    inters = [[] for _ in range(nh)]
    for ch in range(nch):
        r0 = ch * CHUNK
        for h in range(nh):
            c_state = c_scr[h]
            inters[h].append(_dot(p_ref[r0:r0 + CHUNK, h * dk:(h + 1) * dk], c_state.astype(BF16)))
            k_c = p_ref[r0:r0 + CHUNK, nh * dk + h * dk:nh * dk + (h + 1) * dk]
            c_scr[h] = cdecs[h][r0:r0 + 1, :] * c_state + _dot_tn(k_c, xs[h][r0:r0 + CHUNK, :])

    for h in range(nh):
        og = p_ref[:, 2 * nh * dk + nh * dv + h * dv:2 * nh * dk + nh * dv + (h + 1) * dv]
        tot = w_inters[h] * jnp.concatenate(inters[h], axis=0) + intras[h]
        hh = tot[:, :dv] / jnp.maximum(jnp.abs(tot[:, dv:dv + 1]), ems[h])
        y = _sigmoid(og.astype(F32)) * _rms_head(hh, nw_ref[h:h + 1, :])
        o_ref[:, h * dv:(h + 1) * dv] = y.astype(o_ref.dtype)


def _mlstm(proj, gates_t, gate_b, norm_w, bsz, seq):
    t = bsz * seq
    lb = _tile(seq, MIXER_ROWS)
    nb = seq // lb
    bias = jnp.zeros((LANE,), F32).at[LN_MI:LN_MI + 2 * M_HEADS].set(gate_b)
    return pl.pallas_call(
        functools.partial(_mlstm_kernel, lb=lb),
        grid=(bsz, nb),
        in_specs=[
            pl.BlockSpec((lb, M_W), lambda b, c: (b * nb + c, 0)),
            pl.BlockSpec((LANE, lb), lambda b, c: (0, b * nb + c)),
            pl.BlockSpec((LANE, 1), lambda b, c: (0, 0)),
            pl.BlockSpec((M_HEADS, M_DV), lambda b, c: (0, 0)),
        ],
        out_specs=pl.BlockSpec((lb, BRANCH_W), lambda b, c: (b * nb + c, 0)),
        out_shape=jax.ShapeDtypeStruct((t, BRANCH_W), BF16),
        scratch_shapes=[pltpu.VMEM((M_HEADS, M_DK, M_DV + LANE), F32), pltpu.VMEM((SUBLANE, LANE), F32)],
        compiler_params=_cparams("parallel", "arbitrary"),
        name="mlstm",
    )(proj, gates_t, bias.reshape(LANE, 1), norm_w)


def _unit_lower_inverses(a_list, blk_diag):
    n = a_list[0].shape[0]
    r = _iota2((n, n), 0)
    c = _iota2((n, n), 1)
    eye = jnp.where(r == c, 1.0, 0.0).astype(F32)
    base = SUBLANE
    in_base = (r // base) == (c // base)
    nd = [jnp.where(in_base, -a, 0.0).astype(BF16) for a in a_list]
    n2 = [_dot(m, m).astype(BF16) for m in nd]
    xs = [eye + m.astype(F32) for m in nd]
    xs = [x + _dot(x.astype(BF16), m) for x, m in zip(xs, n2)]
    n4 = [_dot(m, m).astype(BF16) for m in n2]
    xs = [x + _dot(x.astype(BF16), m) for x, m in zip(xs, n4)]
    blk = base
    while blk < blk_diag:
        sel = ((r // (2 * blk)) == (c // (2 * blk))) & ((r // blk) != (c // blk))
        offs = [jnp.where(sel, a, 0.0).astype(BF16) for a in a_list]
        xb = [x.astype(BF16) for x in xs]
        xa = [_dot(x, o).astype(BF16) for x, o in zip(xb, offs)]
        xs = [x - _dot(m, x_b) for x, m, x_b in zip(xs, xa, xb)]
        blk *= 2
    return xs


def _gdn_kernel(p_ref, z_ref, g_ref, gt_ref, pcol_ref, prow_ref, nw_ref, o_ref,
                s_scr, wq_scr, u_scr, kd_scr, pm_scr, qs_scr, *, lb):
    nh, dk, dv = DN_HEADS, DN_DK, DN_DV
    scale = dk ** -0.5

    @pl.when(pl.program_id(1) == 0)
    def _():
        s_scr[...] = jnp.zeros_like(s_scr)

    gc = g_ref[...]
    beta_col = _sigmoid(gc)
    gl_col = -jnp.exp(pcol_ref[0:1, :]) * _softplus(gc + pcol_ref[1:2, :])
    row0 = LN_DA % SUBLANE
    rbase = LN_DA - row0
    gr = gt_ref[rbase:rbase + 2 * SUBLANE, :]
    pr = prow_ref[rbase:rbase + 2 * SUBLANE, :]
    gl_row = -jnp.exp(pr[:, 0:1]) * _softplus(gr + pr[:, 1:2])

    lower, upper = _cumsum_mats(lb, CHUNK)
    r = _iota2((lb, lb), 0)
    c = _iota2((lb, lb), 1)
    same = (r // CHUNK) == (c // CHUNK)
    incl = same & (c <= r)
    strict = same & (c < r)
    ones_bd = jnp.where(same, 1.0, 0.0).astype(F32)
    gcum_col = _sum01(lower, gl_col)
    gtot_col = _sum01(ones_bd, gl_col)
    gcum_row = _sum01_t(gl_row, upper)

    ks, qs, vs, a_list, gccs, bcs = [], [], [], [], [], []
    for h in range(nh):
        qa = p_ref[:, h * dk:(h + 1) * dk]
        ka = p_ref[:, nh * dk + h * dk:nh * dk + (h + 1) * dk]
        va = p_ref[:, 2 * nh * dk + h * dv:2 * nh * dk + (h + 1) * dv]
        gcc = gcum_col[:, LN_DA + h:LN_DA + h + 1]
        gcr = gcum_row[row0 + h:row0 + h + 1, :]
        b_c = beta_col[:, LN_DB + h:LN_DB + h + 1]
        decay = jnp.exp(jnp.where(incl, gcc - gcr, -jnp.inf))
        kq = _dot_nt(jnp.concatenate([(ka.astype(F32) * b_c).astype(BF16), qa], axis=0), ka)
        a_list.append(kq[:lb, :] * jnp.where(strict, decay, 0.0))
        pm_scr[h] = (kq[lb:, :] * (scale * decay)).astype(BF16)
        ks.append(ka.astype(F32)); qs.append(qa.astype(F32)); vs.append(va.astype(F32)); gccs.append(gcc); bcs.append(b_c)
    xs = _unit_lower_inverses(a_list, CHUNK)
    for h in range(nh):
        eg = jnp.exp(gccs[h])
        rhs = jnp.concatenate([ks[h] * (bcs[h] * eg), vs[h] * bcs[h]], axis=1).astype(BF16)
        wu = _dot(xs[h].astype(BF16), rhs)
        w = wu[:, :dk].astype(BF16)
        qg = (qs[h] * (eg * scale)).astype(BF16)
        for ch in range(lb // CHUNK):
            wq_scr[h, 2 * ch * CHUNK:(2 * ch + 1) * CHUNK, :] = w[ch * CHUNK:(ch + 1) * CHUNK, :]
            wq_scr[h, (2 * ch + 1) * CHUNK:(2 * ch + 2) * CHUNK, :] = qg[ch * CHUNK:(ch + 1) * CHUNK, :]
        u_scr[h] = wu[:, dk:]
        gtot = gtot_col[:, LN_DA + h:LN_DA + h + 1]
        kd_scr[h] = (ks[h] * jnp.exp(gtot - gccs[h])).astype(BF16)

    for ch in range(lb // CHUNK):
        r0 = ch * CHUNK
        dec_row = jnp.exp(gtot_col[r0:r0 + 1, :])
        for h in range(nh):
            s_prev = s_scr[h]
            sb = s_prev.astype(BF16)
            ws = _dot(wq_scr[h, 2 * r0:2 * r0 + 2 * CHUNK, :], sb)
            u_new = u_scr[h, r0:r0 + CHUNK, :] - ws[:CHUNK, :]
            qs_scr[h, r0:r0 + CHUNK, :] = ws[CHUNK:, :]
            u_scr[h, r0:r0 + CHUNK, :] = u_new
            s_scr[h] = dec_row[:, LN_DA + h:LN_DA + h + 1] * s_prev + _dot_tn(kd_scr[h, r0:r0 + CHUNK, :], u_new.astype(BF16))

    for h in range(nh):
        o = qs_scr[h] + _dot(pm_scr[h], u_scr[h].astype(BF16))
        z = z_ref[:, h * dv:(h + 1) * dv].astype(F32)
        o_ref[:, h * dv:(h + 1) * dv] = (_rms_head(o, nw_ref[...]) * _silu(z)).astype(o_ref.dtype)


def _gdn(qkv, z, gates, gates_t, a_log, dt_bias, norm_w, bsz, seq):
    t = bsz * seq
    lb = _tile(seq, GDN_ROWS)
    nb = seq // lb
    nh = DN_HEADS
    par = jnp.zeros((2, LANE), F32)
    par = par.at[0, LN_DA:LN_DA + nh].set(a_log).at[1, LN_DA:LN_DA + nh].set(dt_bias)
    return pl.pallas_call(
        functools.partial(_gdn_kernel, lb=lb),
        grid=(bsz, nb),
        in_specs=[
            pl.BlockSpec((lb, DQKV_W), lambda b, c: (b * nb + c, 0)),
            pl.BlockSpec((lb, DZ_W), lambda b, c: (b * nb + c, 0)),
            pl.BlockSpec((lb, LANE), lambda b, c: (b * nb + c, 0)),
            pl.BlockSpec((LANE, lb), lambda b, c: (0, b * nb + c)),
            pl.BlockSpec((2, LANE), lambda b, c: (0, 0)),
            pl.BlockSpec((LANE, 2), lambda b, c: (0, 0)),
            pl.BlockSpec((1, DN_DV), lambda b, c: (0, 0)),
        ],
        out_specs=pl.BlockSpec((lb, BRANCH_W), lambda b, c: (b * nb + c, 0)),
        out_shape=jax.ShapeDtypeStruct((t, BRANCH_W), BF16),
        scratch_shapes=[
            pltpu.VMEM((nh, DN_DK, DN_DV), F32),
            pltpu.VMEM((nh, 2 * lb, DN_DK), BF16),
            pltpu.VMEM((nh, lb, DN_DV), F32),
            pltpu.VMEM((nh, lb, DN_DK), BF16),
            pltpu.VMEM((nh, lb, lb), BF16),
            pltpu.VMEM((nh, lb, DN_DV), F32),
        ],
        compiler_params=_cparams("parallel", "arbitrary"),
        name="gdn",
    )(qkv, z, gates, gates_t, par, par.T, norm_w.reshape(1, DN_DV))


def _gla_kernel(p_ref, g_ref, wa_ref, ba_ref, nw_ref, o_ref, s_scr, *, lb):
    nh, dk, dv = G_HEADS, G_DK, G_DV
    scale = dk ** -0.5
    grp = GLA_GROUP
    nsub = grp // G_CHUNK

    @pl.when(pl.program_id(1) == 0)
    def _():
        s_scr[...] = jnp.zeros_like(s_scr)

    log_a = _log_sigmoid(_dot(g_ref[...].astype(BF16), wa_ref[...].astype(BF16)) + ba_ref[...]) * (1.0 / G_TAU)
    lower, _ = _cumsum_mats(grp, G_CHUNK)
    r = _iota2((grp, grp), 0)
    c = _iota2((grp, grp), 1)
    dblk = (r // G_CHUNK) - (c // G_CHUNK)
    ones_blk = jnp.where(dblk == 0, 1.0, 0.0).astype(F32)
    diag_lower = (dblk == 0) & (c <= r)

    for gi in range(lb // grp):
        r0 = gi * grp
        la = log_a[r0:r0 + grp, :]
        lc_all = _sum01(lower, la)
        le_all = _sum01(ones_blk, la)
        before = [_shift_rows(le_all, d * G_CHUNK) for d in range(1, nsub)]
        after = [_shift_rows(le_all, -d * G_CHUNK) for d in range(1, nsub)]
        skipped = [None, None]
        for d in range(2, nsub):
            skipped.append(before[d - 2] if skipped[d - 1] is None else skipped[d - 1] + before[d - 2])
        to_start = functools.reduce(lambda a, b: a + b, before)
        to_end = functools.reduce(lambda a, b: a + b, after)
        total = le_all[0:1, :] + to_end[0:1, :]

        ps, q0s, kends = [], [], []
        for h in range(nh):
            sl = slice(h * dk, (h + 1) * dk)
            q = p_ref[r0:r0 + grp, h * dk:(h + 1) * dk].astype(F32)
            k = p_ref[r0:r0 + grp, nh * dk + h * dk:nh * dk + (h + 1) * dk].astype(F32)
            lc = lc_all[:, sl]
            q_in = q * (jnp.exp(lc) * scale)
            q_b = q_in.astype(BF16)
            k_in = (k * jnp.exp(-lc)).astype(BF16)
            kd = k * jnp.exp(le_all[:, sl] - lc)
            kd_b = kd.astype(BF16)
            p = jnp.where(diag_lower, _dot_nt(q_b, k_in), 0.0)
            p = p + jnp.where(dblk == 1, _dot_nt(q_b, kd_b), 0.0)
            for d in range(2, nsub):
                qd = (q_in * jnp.exp(skipped[d][:, sl])).astype(BF16)
                p = p + jnp.where(dblk == d, _dot_nt(qd, kd_b), 0.0)
            ps.append(p.astype(BF16))
            q0s.append((q_in * jnp.exp(to_start[:, sl])).astype(BF16))
            kends.append((kd * jnp.exp(to_end[:, sl])).astype(BF16))
        outs = []
        for h in range(nh):
            v = p_ref[r0:r0 + grp, 2 * nh * dk + h * dv:2 * nh * dk + (h + 1) * dv]
            st = s_scr[h]
            outs.append(_dot(ps[h], v) + _dot_nt(q0s[h], st.astype(BF16)))
            s_scr[h] = jnp.exp(total[:, h * dk:(h + 1) * dk]) * st + _dot_tn(v, kends[h])
        for h in range(nh):
            rg = p_ref[r0:r0 + grp, 2 * nh * dk + nh * dv + h * dv:2 * nh * dk + nh * dv + (h + 1) * dv]
            y = _rms_head(outs[h], nw_ref[...]) * _silu(rg.astype(F32))
            o_ref[r0:r0 + grp, h * dv:(h + 1) * dv] = y.astype(o_ref.dtype)


def _gla(proj, gates, w_a2, b_a, norm_w, bsz, seq):
    t = bsz * seq
    lb = _tile(seq, GLA_ROWS)
    nb = seq // lb
    gw = G_HEADS * G_DK
    wa = jnp.zeros((LANE, gw), F32).at[LN_GLR:LN_GLR + G_RANK, :].set(w_a2)
    return pl.pallas_call(
        functools.partial(_gla_kernel, lb=lb),
        grid=(bsz, nb),
        in_specs=[
            pl.BlockSpec((lb, G_W), lambda b, c: (b * nb + c, 0)),
            pl.BlockSpec((lb, LANE), lambda b, c: (b * nb + c, 0)),
            pl.BlockSpec((LANE, gw), lambda b, c: (0, 0)),
            pl.BlockSpec((1, gw), lambda b, c: (0, 0)),
            pl.BlockSpec((1, G_DV), lambda b, c: (0, 0)),
        ],
        out_specs=pl.BlockSpec((lb, BRANCH_W), lambda b, c: (b * nb + c, 0)),
        out_shape=jax.ShapeDtypeStruct((t, BRANCH_W), BF16),
        scratch_shapes=[pltpu.VMEM((G_HEADS, G_DV, G_DK), F32)],
        compiler_params=_cparams("parallel", "arbitrary"),
        name="gla",
    )(proj, gates, wa, b_a.reshape(1, gw), norm_w.reshape(1, G_DV))


def _merge_kernel(ym_ref, yd_ref, yg_ref, gm_ref, gd_ref, gg_ref, w_hbm, o_ref, stage, wb, sem, *, tn, layer, nj):
    tm = o_ref.shape[0]

    @pl.when(pl.program_id(1) == 0)
    def _():
        for g in range(N_BRANCH):
            _load_weight_tile(w_hbm.at[layer], stage.at[g], wb.at[g], sem.at[g], layer=g, col0=0, tn=tn, nj=nj, k_minor=False)

    y_refs = (ym_ref, yd_ref, yg_ref)
    g_refs = (gm_ref, gd_ref, gg_ref)
    gw = _group_width(tn)

    def products(r0, tr):
        return [[_dot(y_refs[g][r0:r0 + tr, :], wb[g, :, g0:g0 + gw]) for g in range(N_BRANCH)] for g0 in range(0, tn, gw)]

    def epilogue(r0, tr, raw):
        for n, g0 in enumerate(range(0, tn, gw)):
            acc = _sigmoid(g_refs[0][r0:r0 + tr, g0:g0 + gw].astype(F32)) * raw[n][0]
            for g in range(1, N_BRANCH):
                acc = acc + _sigmoid(g_refs[g][r0:r0 + tr, g0:g0 + gw].astype(F32)) * raw[n][g]
            o_ref[r0:r0 + tr, g0:g0 + gw] = acc.astype(o_ref.dtype)

    _pipelined_pieces(tm, MERGE_HALVINGS, products, epilogue)


def _merge(y_m, y_d, y_g, w_branch, layer, merge_pre, d_model):
    t = y_m.shape[0]
    tm = _tile(t, ROW_TILE)
    tn = _tile(d_model, COL_TILE_FFN)
    nj = d_model // tn
    yspec = pl.BlockSpec((tm, BRANCH_W), lambda j, i: (i, 0))
    gspec = lambda g: pl.BlockSpec((tm, tn), lambda j, i: (i, g * nj + j))
    return pl.pallas_call(
        functools.partial(_merge_kernel, tn=tn, layer=layer, nj=nj),
        grid=(nj, t // tm),
        in_specs=[yspec, yspec, yspec, gspec(0), gspec(1), gspec(2), pl.BlockSpec(memory_space=pl.ANY)],
        out_specs=pl.BlockSpec((tm, tn), lambda j, i: (i, j)),
        out_shape=jax.ShapeDtypeStruct((t, d_model), BF16),
        scratch_shapes=[pltpu.VMEM((N_BRANCH, BRANCH_W, tn), F32), pltpu.VMEM((N_BRANCH, BRANCH_W, tn), BF16),
                        pltpu.SemaphoreType.DMA((N_BRANCH,))],
        compiler_params=_cparams("arbitrary", "arbitrary"),
        name="merge",
    )(y_m, y_d, y_g, merge_pre, merge_pre, merge_pre, w_branch)


def _ffn_up_kernel(h_ref, w_hbm, cg_ref, cu_ref, o_ref, stage, wb, sem, tail_g, tail_u, *, tn, d_ff, rows_per_seq, layer, nj):
    i = pl.program_id(1)
    tm = h_ref.shape[0]

    @pl.when(i == 0)
    def _():
        for m, col0 in enumerate((0, d_ff)):
            _load_weight_tile(w_hbm, stage.at[m], wb.at[m], sem.at[m], layer=layer, col0=col0, tn=tn, nj=nj, k_minor=False)

    @pl.when((i * tm) % rows_per_seq == 0)
    def _():
        tail_g[...] = jnp.zeros_like(tail_g)
        tail_u[...] = jnp.zeros_like(tail_u)

    gw = _group_width(tn)

    def products(r0, tr):
        h = h_ref[r0:r0 + tr, :]
        return [(_dot(h, wb[0, :, g0:g0 + gw]), _dot(h, wb[1, :, g0:g0 + gw])) for g0 in range(0, tn, gw)]

    def epilogue(r0, tr, raw):
        for n, g0 in enumerate(range(0, tn, gw)):
            gate = _causal_conv(raw[n][0], tail_g, g0, cg_ref, FFN_CONV)
            up = _causal_conv(raw[n][1], tail_u, g0, cu_ref, FFN_CONV)
            o_ref[r0:r0 + tr, g0:g0 + gw] = (_silu(gate) * up).astype(o_ref.dtype)

    _pipelined_pieces(tm, FFN_HALVINGS, products, epilogue)


def _ffn_up(h, w_up, conv_w, layer, seq):
    t, k = h.shape
    d_ff = w_up.shape[-1] // 2
    tm = _tile(seq, ROW_TILE)
    tn = _tile(d_ff, COL_TILE_FFN)
    nj = d_ff // tn
    return pl.pallas_call(
        functools.partial(_ffn_up_kernel, tn=tn, d_ff=d_ff, rows_per_seq=seq, layer=layer, nj=nj),
        grid=(nj, t // tm),
        in_specs=[
            pl.BlockSpec((tm, k), lambda j, i: (i, 0)),
            pl.BlockSpec(memory_space=pl.ANY),
            pl.BlockSpec((None, FFN_CONV, tn), lambda j, i: (layer, 0, j)),
            pl.BlockSpec((None, FFN_CONV, tn), lambda j, i: (layer, 0, nj + j)),
        ],
        out_specs=pl.BlockSpec((tm, tn), lambda j, i: (i, j)),
        out_shape=jax.ShapeDtypeStruct((t, d_ff), BF16),
        scratch_shapes=[pltpu.VMEM((2, k, tn), F32), pltpu.VMEM((2, k, tn), BF16), pltpu.SemaphoreType.DMA((2,)),
                        pltpu.VMEM((SUBLANE, tn), F32), pltpu.VMEM((SUBLANE, tn), F32)],
        compiler_params=_cparams("arbitrary", "arbitrary"),
        name="ffn_up",
    )(h, w_up, conv_w, conv_w)


def kernel(x, norm_mix_pre, norm_mix_post, norm_ffn_pre, norm_ffn_post, w_in, mlstm_gate_b, mlstm_norm, dn_conv, dn_a_log, dn_dt_bias, dn_norm, gla_w_a2, gla_b_a, gla_norm, w_branch, w_out, w_ffn_up, ffn_conv, w_ffn_down):
    bsz, seq, d_model = x.shape
    depth = w_in.shape[0]
    t = bsz * seq
    x2 = x.reshape(t, d_model)
    wt = jnp.transpose(w_in, (2, 0, 1))
    proj = functools.partial(_wsmm, k_minor=True)
    dense = functools.partial(_wsmm, k_minor=False)
    h = _norm_cast(x2, norm_mix_pre[0])
    for l in range(depth):
        proj_m = proj(h, wt, l, OFF_M, M_W, BF16, tn=COL_TILE_SLAB, name="proj_m")
        d_qkv = proj(h, wt, l, OFF_D, DQKV_W, BF16, tn=COL_TILE_SLAB, name="proj_dqkv",
                     conv=dn_conv[l], seq=seq, l2_cols=2 * DN_HEADS * DN_DK)
        d_z = proj(h, wt, l, OFF_DZ, DZ_W, BF16, tn=COL_TILE_SLAB, name="proj_dz")
        proj_g = proj(h, wt, l, OFF_G, G_W, BF16, tn=COL_TILE_SLAB, name="proj_g")
        merge_pre = proj(h, wt, l, OFF_MERGE, N_BRANCH * d_model, BF16, tn=COL_TILE_WIDE, name="proj_merge")
        gates, gates_t = _gates(h, wt, l)
        y_m = _mlstm(proj_m, gates_t, mlstm_gate_b[l], mlstm_norm[l], bsz, seq)
        y_d = _gdn(d_qkv, d_z, gates, gates_t, dn_a_log[l], dn_dt_bias[l], dn_norm[l], bsz, seq)
        y_g = _gla(proj_g, gates, gla_w_a2[l], gla_b_a[l], gla_norm[l], bsz, seq)
        merged = _merge(y_m, y_d, y_g, w_branch, l, merge_pre, d_model)
        mix = dense(merged, w_out, l, 0, d_model, BF16, tn=COL_TILE_WIDE, name="out_proj")
        x2, h = _resid_norm(x2, mix, norm_mix_post[l], norm_ffn_pre[l])
        act = _ffn_up(h, w_ffn_up, ffn_conv, l, seq)
        down = dense(act, w_ffn_down, l, 0, d_model, BF16, tn=COL_TILE_FFN, name="ffn_down")
        x2, h = _resid_norm(x2, down, norm_ffn_post[l], norm_mix_pre[l + 1] if l + 1 < depth else None)
    return x2.reshape(bsz, seq, d_model)
```

```python
import functools

import jax
import jax.numpy as jnp
from jax import lax
from jax.experimental import pallas as pl
from jax.experimental.pallas import tpu as pltpu

F32 = jnp.float32
BF16 = jnp.bfloat16
EPS = 1e-6
LANE = 128
SUBLANE = 8
MXU_WIDTH = 256
VMEM_LIMIT = 56 * 1024 * 1024

M_HEADS, M_DK, M_DV = 6, 128, 256
DN_HEADS, DN_DK, DN_DV, DN_CONV = 12, 128, 128, 4
G_HEADS, G_DK, G_DV, G_RANK, G_TAU = 6, 128, 256, 16, 16.0
CHUNK, G_CHUNK = 64, 16
N_BRANCH, BRANCH_W, FFN_CONV = 3, 1536, 3

M_W = 2 * M_HEADS * M_DK + 2 * M_HEADS * M_DV
DQKV_W = 2 * DN_HEADS * DN_DK + DN_HEADS * DN_DV
DZ_W = DN_HEADS * DN_DV
G_W = 2 * G_HEADS * G_DK + 2 * G_HEADS * G_DV
OFF_M = 0
OFF_MGATE = OFF_M + M_W
OFF_D = OFF_MGATE + 2 * M_HEADS
OFF_DZ = OFF_D + DQKV_W
OFF_DGATE = OFF_DZ + DZ_W
OFF_G = OFF_DGATE + 2 * DN_HEADS
OFF_GLR = OFF_G + G_W
OFF_MERGE = OFF_GLR + G_RANK
GATE_SRC = ((OFF_MGATE, 0, 2 * SUBLANE), (OFF_DGATE, 2 * SUBLANE, 2 * DN_HEADS), (OFF_GLR, 2 * SUBLANE + 2 * DN_HEADS, G_RANK))
LN_MI = 0
LN_MF = LN_MI + M_HEADS
LN_DB = GATE_SRC[1][1]
LN_DA = LN_DB + DN_HEADS
LN_GLR = GATE_SRC[2][1]
assert 2 * M_HEADS <= GATE_SRC[0][2] and LN_GLR + G_RANK <= LANE

ROW_TILE = 1024
COL_TILE_SLAB = 768
COL_TILE_WIDE = 1024
COL_TILE_FFN = 512
EPILOGUE_HALVINGS = 1
FFN_HALVINGS = 1
MERGE_HALVINGS = 2
MIXER_ROWS = 256
GDN_ROWS = 128
GLA_ROWS = 256
GLA_GROUP = 64


def _cparams(*sem):
    return pltpu.CompilerParams(dimension_semantics=sem, vmem_limit_bytes=VMEM_LIMIT)


def _dot(a, b):
    return jnp.dot(a, b, preferred_element_type=F32)


def _dot_nt(a, b):
    return lax.dot_general(a, b, (((1,), (1,)), ((), ())), preferred_element_type=F32)


def _dot_tn(a, b):
    return lax.dot_general(a, b, (((0,), (0,)), ((), ())), preferred_element_type=F32)


def _split3(x):
    hi = x.astype(BF16)
    r1 = x - hi.astype(F32)
    mid = r1.astype(BF16)
    lo = (r1 - mid.astype(F32)).astype(BF16)
    return hi, mid, lo


def _sum01(m01, x):
    m = m01.astype(BF16)
    hi, mid, lo = _split3(x)
    return _dot(m, hi) + _dot(m, mid) + _dot(m, lo)


def _sum01_t(x, m01):
    m = m01.astype(BF16)
    hi, mid, lo = _split3(x)
    return _dot(hi, m) + _dot(mid, m) + _dot(lo, m)


def _sigmoid(x):
    return 1.0 / (1.0 + jnp.exp(-x))


def _silu(x):
    return x * _sigmoid(x)


def _log_sigmoid(x):
    return jnp.minimum(x, 0.0) - jnp.log(1.0 + jnp.exp(-jnp.abs(x)))


def _softplus(x):
    return jnp.maximum(x, 0.0) + jnp.log(1.0 + jnp.exp(-jnp.abs(x)))


def _iota2(shape, axis):
    return lax.broadcasted_iota(jnp.int32, shape, axis)


def _tile(n, pref):
    t = min(n, pref)
    assert n % t == 0, (n, pref)
    return t


def _group_width(tn):
    return MXU_WIDTH if tn % MXU_WIDTH == 0 else tn


def _norm_cast_kernel(x_ref, g_ref, o_ref):
    x = x_ref[...]
    y = x * lax.rsqrt(jnp.mean(x * x, axis=-1, keepdims=True) + EPS)
    o_ref[...] = (y * g_ref[...]).astype(o_ref.dtype)


def _norm_cast(x2, g):
    t, d = x2.shape
    tr = _tile(t, 256)
    return pl.pallas_call(
        _norm_cast_kernel,
        grid=(t // tr,),
        in_specs=[pl.BlockSpec((tr, d), lambda i: (i, 0)), pl.BlockSpec((1, d), lambda i: (0, 0))],
        out_specs=pl.BlockSpec((tr, d), lambda i: (i, 0)),
        out_shape=jax.ShapeDtypeStruct((t, d), BF16),
        compiler_params=_cparams("parallel"),
        name="norm_cast",
    )(x2, g.reshape(1, d))


def _resid_norm_kernel(x_ref, y_ref, gp_ref, gn_ref, xo_ref, ho_ref):
    y = y_ref[...].astype(F32)
    yn = y * lax.rsqrt(jnp.mean(y * y, axis=-1, keepdims=True) + EPS) * gp_ref[...]
    xn = x_ref[...] + yn
    xo_ref[...] = xn
    hn = xn * lax.rsqrt(jnp.mean(xn * xn, axis=-1, keepdims=True) + EPS)
    ho_ref[...] = (hn * gn_ref[...]).astype(ho_ref.dtype)


def _resid_kernel(x_ref, y_ref, gp_ref, xo_ref):
    y = y_ref[...].astype(F32)
    yn = y * lax.rsqrt(jnp.mean(y * y, axis=-1, keepdims=True) + EPS) * gp_ref[...]
    xo_ref[...] = x_ref[...] + yn


def _resid_norm(x2, y, g_post, g_next):
    t, d = x2.shape
    tr = _tile(t, 256)
    row = pl.BlockSpec((tr, d), lambda i: (i, 0))
    vec = pl.BlockSpec((1, d), lambda i: (0, 0))
    if g_next is None:
        return pl.pallas_call(
            _resid_kernel,
            grid=(t // tr,),
            in_specs=[row, row, vec],
            out_specs=row,
            out_shape=jax.ShapeDtypeStruct((t, d), F32),
            compiler_params=_cparams("parallel"),
            name="resid",
        )(x2, y, g_post.reshape(1, d)), None
    return pl.pallas_call(
        _resid_norm_kernel,
        grid=(t // tr,),
        in_specs=[row, row, vec, vec],
        out_specs=[row, row],
        out_shape=[jax.ShapeDtypeStruct((t, d), F32), jax.ShapeDtypeStruct((t, d), BF16)],
        compiler_params=_cparams("parallel"),
        name="resid_norm",
    )(x2, y, g_post.reshape(1, d), g_next.reshape(1, d))


def _weight_window(w_hbm, layer, col, tn, k_minor):
    if k_minor:
        return w_hbm.at[pl.ds(col, tn), layer, :]
    return w_hbm.at[layer, :, pl.ds(pl.multiple_of(col, LANE), tn)]


def _weight_tile_copy(w_hbm, stage, sem, jj, *, layer, col0, tn, k_minor):
    return pltpu.make_async_copy(_weight_window(w_hbm, layer, col0 + jj * tn, tn, k_minor), stage, sem)


def _weight_tile_arrive(w_hbm, stage, sem, *, nj, **tile):
    del nj
    j = pl.program_id(0)

    @pl.when(j == 0)
    def _():
        _weight_tile_copy(w_hbm, stage, sem, 0, **tile).start()

    _weight_tile_copy(w_hbm, stage, sem, j, **tile).wait()


def _weight_tile_prefetch(w_hbm, stage, sem, *, nj, **tile):
    j = pl.program_id(0)

    @pl.when(j + 1 < nj)
    def _():
        _weight_tile_copy(w_hbm, stage, sem, j + 1, **tile).start()


def _tile_dot(h, wb, g0, gw, k_minor):
    if k_minor:
        return _dot_nt(h, wb[g0:g0 + gw, :])
    return _dot(h, wb[:, g0:g0 + gw])


def _wsmm_kernel(h_ref, w_hbm, o_ref, stage, wb, sem, *, tn, k_minor, **tile):
    gw = _group_width(tn)

    def step(first):
        if first:
            _weight_tile_arrive(w_hbm, stage, sem, tn=tn, k_minor=k_minor, **tile)
        h = h_ref[...]
        for g0 in range(0, tn, gw):
            if first:
                sl = (slice(g0, g0 + gw), slice(None)) if k_minor else (slice(None), slice(g0, g0 + gw))
                wb[sl] = stage[sl].astype(BF16)
            o_ref[:, g0:g0 + gw] = _tile_dot(h, wb, g0, gw, k_minor).astype(o_ref.dtype)
        if first:
            _weight_tile_prefetch(w_hbm, stage, sem, tn=tn, k_minor=k_minor, **tile)

    first_row_tile = pl.program_id(1) == 0
    pl.when(first_row_tile)(functools.partial(step, True))
    pl.when(jnp.logical_not(first_row_tile))(functools.partial(step, False))


def _causal_conv(x, tail_ref, g0, w_ref, taps):
    tm, gw = x.shape
    xp = jnp.concatenate([tail_ref[:, g0:g0 + gw], x], axis=0)
    tail_ref[:, g0:g0 + gw] = x[tm - SUBLANE:, :]
    acc = w_ref[taps - 1:taps, g0:g0 + gw] * x
    for d in range(1, taps):
        acc = acc + w_ref[taps - 1 - d:taps - d, g0:g0 + gw] * pltpu.roll(xp, d, 0)[SUBLANE:, :]
    return acc


def _row_pieces(tm, halvings):
    sizes = [tm >> (n + 1) for n in range(halvings)]
    sizes.append(tm - sum(sizes))
    assert all(s % SUBLANE == 0 and s > 0 for s in sizes), sizes
    starts = [sum(sizes[:n]) for n in range(len(sizes))]
    return list(zip(starts, sizes))


def _pipelined_pieces(tm, halvings, products, epilogue):
    pending = None
    for r0, tr in _row_pieces(tm, halvings):
        raw = products(r0, tr)
        if pending is not None:
            epilogue(*pending)
        pending = (r0, tr, raw)
    epilogue(*pending)


def _wsmm_conv_kernel(h_ref, w_hbm, cw_ref, o_ref, stage, wb, sem, tail, *, tn, k_minor, rows_per_seq, l2_tiles, **tile):
    j = pl.program_id(0)
    i = pl.program_id(1)
    tm = h_ref.shape[0]

    @pl.when((i * tm) % rows_per_seq == 0)
    def _():
        tail[...] = jnp.zeros_like(tail)

    gw = _group_width(tn)

    def epilogue(r0, tr, raw):
        for n, g0 in enumerate(range(0, tn, gw)):
            y = _silu(_causal_conv(raw[n], tail, g0, cw_ref, DN_CONV))
            for s0 in range(0, gw, DN_DK):
                ys = y[:, s0:s0 + DN_DK]
                yn = ys * lax.rsqrt(jnp.sum(ys * ys, axis=-1, keepdims=True) + EPS)
                o_ref[r0:r0 + tr, g0 + s0:g0 + s0 + DN_DK] = jnp.where(j < l2_tiles, yn, ys).astype(o_ref.dtype)

    def step(first):
        if first:
            _weight_tile_arrive(w_hbm, stage, sem, tn=tn, k_minor=k_minor, **tile)

        def products(r0, tr):
            h = h_ref[r0:r0 + tr, :]
            out = []
            for g0 in range(0, tn, gw):
                if first and r0 == 0:
                    sl = (slice(g0, g0 + gw), slice(None)) if k_minor else (slice(None), slice(g0, g0 + gw))
                    wb[sl] = stage[sl].astype(BF16)
                out.append(_tile_dot(h, wb, g0, gw, k_minor))
            return out

        _pipelined_pieces(tm, EPILOGUE_HALVINGS, products, epilogue)
        if first:
            _weight_tile_prefetch(w_hbm, stage, sem, tn=tn, k_minor=k_minor, **tile)

    pl.when(i == 0)(functools.partial(step, True))
    pl.when(i != 0)(functools.partial(step, False))


def _wsmm(h, w, layer, col0, n, out_dtype, *, tn, k_minor, name, conv=None, seq=None, l2_cols=0):
    t, k = h.shape
    tm = _tile(t, ROW_TILE if seq is None else min(ROW_TILE, seq))
    tn = _tile(n, tn)
    nj = n // tn
    tile = dict(layer=layer, col0=col0, nj=nj)
    wshape = (tn, k) if k_minor else (k, tn)
    in_specs = [pl.BlockSpec((tm, k), lambda j, i: (i, 0)), pl.BlockSpec(memory_space=pl.ANY)]
    scratch = [pltpu.VMEM(wshape, F32), pltpu.VMEM(wshape, BF16), pltpu.SemaphoreType.DMA(())]
    args = [h, w]
    if conv is None:
        body = functools.partial(_wsmm_kernel, tn=tn, k_minor=k_minor, **tile)
    else:
        assert seq % tm == 0 and l2_cols % tn == 0
        body = functools.partial(_wsmm_conv_kernel, tn=tn, k_minor=k_minor, rows_per_seq=seq, l2_tiles=l2_cols // tn, **tile)
        in_specs.append(pl.BlockSpec((conv.shape[0], tn), lambda j, i: (0, j)))
        scratch.append(pltpu.VMEM((SUBLANE, tn), F32))
        args.append(conv)
    return pl.pallas_call(
        body,
        grid=(nj, t // tm),
        in_specs=in_specs,
        out_specs=pl.BlockSpec((tm, tn), lambda j, i: (i, j)),
        out_shape=jax.ShapeDtypeStruct((t, n), out_dtype),
        scratch_shapes=scratch,
        compiler_params=_cparams("arbitrary", "arbitrary"),
        name=name,
    )(*args)


def _gates_kernel(h_ref, wt_hbm, g_ref, gt_ref, stage, wg, sem, *, layer):
    @pl.when(pl.program_id(0) == 0)
    def _():
        stage[...] = jnp.zeros_like(stage)
        copies = [pltpu.make_async_copy(wt_hbm.at[pl.ds(src, rows), layer, :], stage.at[pl.ds(dst, rows), :], sem.at[n])
                  for n, (src, dst, rows) in enumerate(GATE_SRC)]
        for cp in copies:
            cp.start()
        for cp in copies:
            cp.wait()
        wg[...] = stage[...].astype(BF16)

    gt = _dot_nt(wg[...], h_ref[...])
    gt_ref[...] = gt
    g_ref[...] = gt.T


def _gates(h, wt, layer):
    t, k = h.shape
    tm = _tile(t, 512)
    return pl.pallas_call(
        functools.partial(_gates_kernel, layer=layer),
        grid=(t // tm,),
        in_specs=[pl.BlockSpec((tm, k), lambda i: (i, 0)), pl.BlockSpec(memory_space=pl.ANY)],
        out_specs=[pl.BlockSpec((tm, LANE), lambda i: (i, 0)), pl.BlockSpec((LANE, tm), lambda i: (0, i))],
        out_shape=[jax.ShapeDtypeStruct((t, LANE), F32), jax.ShapeDtypeStruct((LANE, t), F32)],
        scratch_shapes=[pltpu.VMEM((LANE, k), F32), pltpu.VMEM((LANE, k), BF16), pltpu.SemaphoreType.DMA((len(GATE_SRC),))],
        compiler_params=_cparams("arbitrary"),
        name="gates",
    )(h, wt)


def _cumsum_mats(n, blk):
    r = _iota2((n, n), 0)
    c = _iota2((n, n), 1)
    same = (r // blk) == (c // blk)
    lower = jnp.where(same & (c <= r), 1.0, 0.0).astype(F32)
    upper = jnp.where(same & (r <= c), 1.0, 0.0).astype(F32)
    return lower, upper


def _shift_rows(x, n):
    z = jnp.zeros((abs(n), x.shape[1]), x.dtype)
    if n > 0:
        return jnp.concatenate([z, x[:x.shape[0] - n, :]], axis=0)
    return jnp.concatenate([x[-n:, :], z], axis=0)


def _rms_head(x, w_row):
    return x * lax.rsqrt(jnp.mean(x * x, axis=-1, keepdims=True) + EPS) * w_row


def _mlstm_kernel(p_ref, gt_ref, brow_ref, nw_ref, o_ref, c_scr, m_scr, *, lb):
    nh, dk, dv = M_HEADS, M_DK, M_DV
    scale = dk ** -0.5

    @pl.when(pl.program_id(1) == 0)
    def _():
        c_scr[...] = jnp.zeros_like(c_scr)
        m_scr[...] = jnp.zeros_like(m_scr)

    rows = 2 * SUBLANE
    grow = gt_ref[0:rows, :] + brow_ref[0:rows, :]
    frow = _log_sigmoid(grow)
    nch = lb // CHUNK
    _, upper = _cumsum_mats(lb, CHUNK)
    r = _iota2((lb, lb), 0)
    c = _iota2((lb, lb), 1)
    same = (r // CHUNK) == (c // CHUNK)
    causal = same & (c <= r)
    ones_bd = jnp.where(same, 1.0, 0.0).astype(F32)
    bcum_row = _sum01_t(frow, upper)
    btot_row = _sum01_t(frow, ones_bd)
    tcols = jnp.concatenate([bcum_row, btot_row, grow, jnp.zeros((LANE - 3 * rows, lb), F32)], axis=0).T
    bcum_col = tcols[:, 0:rows]
    btot_col = tcols[:, rows:2 * rows]
    gcol = tcols[:, 2 * rows:3 * rows]
    one_col = jnp.where(_iota2((lb, LANE), 1) == 0, 1.0, 0.0).astype(F32)

    cols = []
    for h in range(nh):
        b_c = bcum_col[:, LN_MF + h:LN_MF + h + 1]
        bt_c = btot_col[:, LN_MF + h:LN_MF + h + 1]
        log_w = bt_c - b_c + gcol[:, LN_MI + h:LN_MI + h + 1]
        m = m_scr[h:h + 1, 0:1]
        mp_rows, mn_rows = [], []
        for ch in range(nch):
            r0 = ch * CHUNK
            mp_rows.append(jnp.broadcast_to(m, (CHUNK, 1)))
            m = jnp.maximum(bt_c[r0:r0 + 1, :] + m, jnp.max(log_w[r0:r0 + CHUNK, :], axis=0, keepdims=True))
            mn_rows.append(jnp.broadcast_to(m, (CHUNK, 1)))
        m_scr[h:h + 1, :] = jnp.broadcast_to(m, (1, LANE))
        cols.append((b_c, bt_c, log_w, jnp.concatenate(mp_rows, axis=0), jnp.concatenate(mn_rows, axis=0)))

    ss, w_inters, ems = [], [], []
    for h in range(nh):
        q = p_ref[:, h * dk:(h + 1) * dk]
        k = p_ref[:, nh * dk + h * dk:nh * dk + (h + 1) * dk]
        b_c, bt_c, log_w, mp_col, mn_col = cols[h]
        b_r = bcum_row[LN_MF + h:LN_MF + h + 1, :]
        i_r = grow[LN_MI + h:LN_MI + h + 1, :]
        log_d = jnp.where(causal, b_c - b_r + i_r, -jnp.inf)
        m_inter = b_c + mp_col
        m_t = jnp.maximum(m_inter, jnp.max(log_d, axis=-1, keepdims=True))
        ss.append((_dot_nt(q, k) * (scale * jnp.exp(log_d - m_t))).astype(BF16))
        w_inters.append(jnp.exp(m_inter - m_t) * scale)
        ems.append(jnp.exp(-m_t))

    intras, xs, cdecs = [], [], []
    for h in range(nh):
        v = p_ref[:, 2 * nh * dk + h * dv:2 * nh * dk + (h + 1) * dv]
        b_c, bt_c, log_w, mp_col, mn_col = cols[h]
        v_ext = jnp.concatenate([v.astype(F32), one_col], axis=1)
        intras.append(_dot(ss[h], v_ext.astype(BF16)))
        xs.append((jnp.exp(log_w - mn_col) * v_ext).astype(BF16))
        cdecs.append(jnp.exp(bt_c + mp_col - mn_col))

    inters = [[] for _ in range(nh)]
    for ch in range(nch):
        r0 = ch * CHUNK
        for h in range(nh):
            c_state = c_scr[h]
            inters[h].append(_dot(p_ref[r0:r0 + CHUNK, h * dk:(h + 1) * dk], c_state.astype(BF16)))
            k_c = p_ref[r0:r0 + CHUNK, nh * dk + h * dk:nh * dk + (h + 1) * dk]
            c_scr[h] = cdecs[h][r0:r0 + 1, :] * c_state + _dot_tn(k_c, xs[h][r0:r0 + CHUNK, :])

    for h in range(nh):
        og = p_ref[:, 2 * nh * dk + nh * dv + h * dv:2 * nh * dk + nh * dv + (h + 1) * dv]
        tot = w_inters[h] * jnp.concatenate(inters[h], axis=0) + intras[h]
        hh = tot[:, :dv] / jnp.maximum(jnp.abs(tot[:, dv:dv + 1]), ems[h])
        y = _sigmoid(og.astype(F32)) * _rms_head(hh, nw_ref[h:h + 1, :])
        o_ref[:, h * dv:(h + 1) * dv] = y.astype(o_ref.dtype)


def _mlstm(proj, gates_t, gate_b, norm_w, bsz, seq):
    t = bsz * seq
    lb = _tile(seq, MIXER_ROWS)
    nb = seq // lb
    bias = jnp.zeros((LANE,), F32).at[LN_MI:LN_MI + 2 * M_HEADS].set(gate_b)
    return pl.pallas_call(
        functools.partial(_mlstm_kernel, lb=lb),
        grid=(bsz, nb),
        in_specs=[
            pl.BlockSpec((lb, M_W), lambda b, c: (b * nb + c, 0)),
            pl.BlockSpec((LANE, lb), lambda b, c: (0, b * nb + c)),
            pl.BlockSpec((LANE, 1), lambda b, c: (0, 0)),
            pl.BlockSpec((M_HEADS, M_DV), lambda b, c: (0, 0)),
        ],
        out_specs=pl.BlockSpec((lb, BRANCH_W), lambda b, c: (b * nb + c, 0)),
        out_shape=jax.ShapeDtypeStruct((t, BRANCH_W), BF16),
        scratch_shapes=[pltpu.VMEM((M_HEADS, M_DK, M_DV + LANE), F32), pltpu.VMEM((SUBLANE, LANE), F32)],
        compiler_params=_cparams("parallel", "arbitrary"),
        name="mlstm",
    )(proj, gates_t, bias.reshape(LANE, 1), norm_w)


def _unit_lower_inverses(a_list, blk_diag):
    n = a_list[0].shape[0]
    r = _iota2((n, n), 0)
    c = _iota2((n, n), 1)
    eye = jnp.where(r == c, 1.0, 0.0).astype(F32)
    base = SUBLANE
    in_base = (r // base) == (c // base)
    nd = [jnp.where(in_base, -a, 0.0).astype(BF16) for a in a_list]
    n2 = [_dot(m, m).astype(BF16) for m in nd]
    xs = [eye + m.astype(F32) for m in nd]
    xs = [x + _dot(x.astype(BF16), m) for x, m in zip(xs, n2)]
    n4 = [_dot(m, m).astype(BF16) for m in n2]
    xs = [x + _dot(x.astype(BF16), m) for x, m in zip(xs, n4)]
    blk = base
    while blk < blk_diag:
        sel = ((r // (2 * blk)) == (c // (2 * blk))) & ((r // blk) != (c // blk))
        offs = [jnp.where(sel, a, 0.0).astype(BF16) for a in a_list]
        xb = [x.astype(BF16) for x in xs]
        xa = [_dot(x, o).astype(BF16) for x, o in zip(xb, offs)]
        xs = [x - _dot(m, x_b) for x, m, x_b in zip(xs, xa, xb)]
        blk *= 2
    return xs


def _gdn_kernel(p_ref, z_ref, g_ref, gt_ref, pcol_ref, prow_ref, nw_ref, o_ref,
                s_scr, wq_scr, u_scr, kd_scr, pm_scr, qs_scr, *, lb):
    nh, dk, dv = DN_HEADS, DN_DK, DN_DV
    scale = dk ** -0.5

    @pl.when(pl.program_id(1) == 0)
    def _():
        s_scr[...] = jnp.zeros_like(s_scr)

    gc = g_ref[...]
    beta_col = _sigmoid(gc)
    gl_col = -jnp.exp(pcol_ref[0:1, :]) * _softplus(gc + pcol_ref[1:2, :])
    row0 = LN_DA % SUBLANE
    rbase = LN_DA - row0
    gr = gt_ref[rbase:rbase + 2 * SUBLANE, :]
    pr = prow_ref[rbase:rbase + 2 * SUBLANE, :]
    gl_row = -jnp.exp(pr[:, 0:1]) * _softplus(gr + pr[:, 1:2])

    lower, upper = _cumsum_mats(lb, CHUNK)
    r = _iota2((lb, lb), 0)
    c = _iota2((lb, lb), 1)
    same = (r // CHUNK) == (c // CHUNK)
    incl = same & (c <= r)
    strict = same & (c < r)
    ones_bd = jnp.where(same, 1.0, 0.0).astype(F32)
    gcum_col = _sum01(lower, gl_col)
    gtot_col = _sum01(ones_bd, gl_col)
    gcum_row = _sum01_t(gl_row, upper)

    ks, qs, vs, a_list, gccs, bcs = [], [], [], [], [], []
    for h in range(nh):
        qa = p_ref[:, h * dk:(h + 1) * dk]
        ka = p_ref[:, nh * dk + h * dk:nh * dk + (h + 1) * dk]
        va = p_ref[:, 2 * nh * dk + h * dv:2 * nh * dk + (h + 1) * dv]
        gcc = gcum_col[:, LN_DA + h:LN_DA + h + 1]
        gcr = gcum_row[row0 + h:row0 + h + 1, :]
        b_c = beta_col[:, LN_DB + h:LN_DB + h + 1]
        decay = jnp.exp(jnp.where(incl, gcc - gcr, -jnp.inf))
        kq = _dot_nt(jnp.concatenate([(ka.astype(F32) * b_c).astype(BF16), qa], axis=0), ka)
        a_list.append(kq[:lb, :] * jnp.where(strict, decay, 0.0))
        pm_scr[h] = (kq[lb:, :] * (scale * decay)).astype(BF16)
        ks.append(ka.astype(F32)); qs.append(qa.astype(F32)); vs.append(va.astype(F32)); gccs.append(gcc); bcs.append(b_c)
    xs = _unit_lower_inverses(a_list, CHUNK)
    for h in range(nh):
        eg = jnp.exp(gccs[h])
        rhs = jnp.concatenate([ks[h] * (bcs[h] * eg), vs[h] * bcs[h]], axis=1).astype(BF16)
        wu = _dot(xs[h].astype(BF16), rhs)
        w = wu[:, :dk].astype(BF16)
        qg = (qs[h] * (eg * scale)).astype(BF16)
        for ch in range(lb // CHUNK):
            wq_scr[h, 2 * ch * CHUNK:(2 * ch + 1) * CHUNK, :] = w[ch * CHUNK:(ch + 1) * CHUNK, :]
            wq_scr[h, (2 * ch + 1) * CHUNK:(2 * ch + 2) * CHUNK, :] = qg[ch * CHUNK:(ch + 1) * CHUNK, :]
        u_scr[h] = wu[:, dk:]
        gtot = gtot_col[:, LN_DA + h:LN_DA + h + 1]
        kd_scr[h] = (ks[h] * jnp.exp(gtot - gccs[h])).astype(BF16)

    for ch in range(lb // CHUNK):
        r0 = ch * CHUNK
        dec_row = jnp.exp(gtot_col[r0:r0 + 1, :])
        for h in range(nh):
            s_prev = s_scr[h]
            sb = s_prev.astype(BF16)
            ws = _dot(wq_scr[h, 2 * r0:2 * r0 + 2 * CHUNK, :], sb)
            u_new = u_scr[h, r0:r0 + CHUNK, :] - ws[:CHUNK, :]
            qs_scr[h, r0:r0 + CHUNK, :] = ws[CHUNK:, :]
            u_scr[h, r0:r0 + CHUNK, :] = u_new
            s_scr[h] = dec_row[:, LN_DA + h:LN_DA + h + 1] * s_prev + _dot_tn(kd_scr[h, r0:r0 + CHUNK, :], u_new.astype(BF16))

    for h in range(nh):
        o = qs_scr[h] + _dot(pm_scr[h], u_scr[h].astype(BF16))
        z = z_ref[:, h * dv:(h + 1) * dv].astype(F32)
        o_ref[:, h * dv:(h + 1) * dv] = (_rms_head(o, nw_ref[...]) * _silu(z)).astype(o_ref.dtype)


def _gdn(qkv, z, gates, gates_t, a_log, dt_bias, norm_w, bsz, seq):
    t = bsz * seq
    lb = _tile(seq, GDN_ROWS)
    nb = seq // lb
    nh = DN_HEADS
    par = jnp.zeros((2, LANE), F32)
    par = par.at[0, LN_DA:LN_DA + nh].set(a_log).at[1, LN_DA:LN_DA + nh].set(dt_bias)
    return pl.pallas_call(
        functools.partial(_gdn_kernel, lb=lb),
        grid=(bsz, nb),
        in_specs=[
            pl.BlockSpec((lb, DQKV_W), lambda b, c: (b * nb + c, 0)),
            pl.BlockSpec((lb, DZ_W), lambda b, c: (b * nb + c, 0)),
            pl.BlockSpec((lb, LANE), lambda b, c: (b * nb + c, 0)),
            pl.BlockSpec((LANE, lb), lambda b, c: (0, b * nb + c)),
            pl.BlockSpec((2, LANE), lambda b, c: (0, 0)),
            pl.BlockSpec((LANE, 2), lambda b, c: (0, 0)),
            pl.BlockSpec((1, DN_DV), lambda b, c: (0, 0)),
        ],
        out_specs=pl.BlockSpec((lb, BRANCH_W), lambda b, c: (b * nb + c, 0)),
        out_shape=jax.ShapeDtypeStruct((t, BRANCH_W), BF16),
        scratch_shapes=[
            pltpu.VMEM((nh, DN_DK, DN_DV), F32),
            pltpu.VMEM((nh, 2 * lb, DN_DK), BF16),
            pltpu.VMEM((nh, lb, DN_DV), F32),
            pltpu.VMEM((nh, lb, DN_DK), BF16),
            pltpu.VMEM((nh, lb, lb), BF16),
            pltpu.VMEM((nh, lb, DN_DV), F32),
        ],
        compiler_params=_cparams("parallel", "arbitrary"),
        name="gdn",
    )(qkv, z, gates, gates_t, par, par.T, norm_w.reshape(1, DN_DV))


def _gla_kernel(p_ref, g_ref, wa_ref, ba_ref, nw_ref, o_ref, s_scr, *, lb):
    nh, dk, dv = G_HEADS, G_DK, G_DV
    scale = dk ** -0.5
    grp = GLA_GROUP
    nsub = grp // G_CHUNK

    @pl.when(pl.program_id(1) == 0)
    def _():
        s_scr[...] = jnp.zeros_like(s_scr)

    log_a = _log_sigmoid(_dot(g_ref[...].astype(BF16), wa_ref[...].astype(BF16)) + ba_ref[...]) * (1.0 / G_TAU)
    lower, _ = _cumsum_mats(grp, G_CHUNK)
    r = _iota2((grp, grp), 0)
    c = _iota2((grp, grp), 1)
    dblk = (r // G_CHUNK) - (c // G_CHUNK)
    ones_blk = jnp.where(dblk == 0, 1.0, 0.0).astype(F32)
    diag_lower = (dblk == 0) & (c <= r)

    for gi in range(lb // grp):
        r0 = gi * grp
        la = log_a[r0:r0 + grp, :]
        lc_all = _sum01(lower, la)
        le_all = _sum01(ones_blk, la)
        before = [_shift_rows(le_all, d * G_CHUNK) for d in range(1, nsub)]
        after = [_shift_rows(le_all, -d * G_CHUNK) for d in range(1, nsub)]
        skipped = [None, None]
        for d in range(2, nsub):
            skipped.append(before[d - 2] if skipped[d - 1] is None else skipped[d - 1] + before[d - 2])
        to_start = functools.reduce(lambda a, b: a + b, before)
        to_end = functools.reduce(lambda a, b: a + b, after)
        total = le_all[0:1, :] + to_end[0:1, :]

        ps, q0s, kends = [], [], []
        for h in range(nh):
            sl = slice(h * dk, (h + 1) * dk)
            q = p_ref[r0:r0 + grp, h * dk:(h + 1) * dk].astype(F32)
            k = p_ref[r0:r0 + grp, nh * dk + h * dk:nh * dk + (h + 1) * dk].astype(F32)
            lc = lc_all[:, sl]
            q_in = q * (jnp.exp(lc) * scale)
            q_b = q_in.astype(BF16)
            k_in = (k * jnp.exp(-lc)).astype(BF16)
            kd = k * jnp.exp(le_all[:, sl] - lc)
            kd_b = kd.astype(BF16)
            p = jnp.where(diag_lower, _dot_nt(q_b, k_in), 0.0)
            p = p + jnp.where(dblk == 1, _dot_nt(q_b, kd_b), 0.0)
            for d in range(2, nsub):
                qd = (q_in * jnp.exp(skipped[d][:, sl])).astype(BF16)
                p = p + jnp.where(dblk == d, _dot_nt(qd, kd_b), 0.0)
            ps.append(p.astype(BF16))
            q0s.append((q_in * jnp.exp(to_start[:, sl])).astype(BF16))
            kends.append((kd * jnp.exp(to_end[:, sl])).astype(BF16))
        outs = []
        for h in range(nh):
            v = p_ref[r0:r0 + grp, 2 * nh * dk + h * dv:2 * nh * dk + (h + 1) * dv]
            st = s_scr[h]
            outs.append(_dot(ps[h], v) + _dot_nt(q0s[h], st.astype(BF16)))
            s_scr[h] = jnp.exp(total[:, h * dk:(h + 1) * dk]) * st + _dot_tn(v, kends[h])
        for h in range(nh):
            rg = p_ref[r0:r0 + grp, 2 * nh * dk + nh * dv + h * dv:2 * nh * dk + nh * dv + (h + 1) * dv]
            y = _rms_head(outs[h], nw_ref[...]) * _silu(rg.astype(F32))
            o_ref[r0:r0 + grp, h * dv:(h + 1) * dv] = y.astype(o_ref.dtype)


def _gla(proj, gates, w_a2, b_a, norm_w, bsz, seq):
    t = bsz * seq
    lb = _tile(seq, GLA_ROWS)
    nb = seq // lb
    gw = G_HEADS * G_DK
    wa = jnp.zeros((LANE, gw), F32).at[LN_GLR:LN_GLR + G_RANK, :].set(w_a2)
    return pl.pallas_call(
        functools.partial(_gla_kernel, lb=lb),
        grid=(bsz, nb),
        in_specs=[
            pl.BlockSpec((lb, G_W), lambda b, c: (b * nb + c, 0)),
            pl.BlockSpec((lb, LANE), lambda b, c: (b * nb + c, 0)),
            pl.BlockSpec((LANE, gw), lambda b, c: (0, 0)),
            pl.BlockSpec((1, gw), lambda b, c: (0, 0)),
            pl.BlockSpec((1, G_DV), lambda b, c: (0, 0)),
        ],
        out_specs=pl.BlockSpec((lb, BRANCH_W), lambda b, c: (b * nb + c, 0)),
        out_shape=jax.ShapeDtypeStruct((t, BRANCH_W), BF16),
        scratch_shapes=[pltpu.VMEM((G_HEADS, G_DV, G_DK), F32)],
        compiler_params=_cparams("parallel", "arbitrary"),
        name="gla",
    )(proj, gates, wa, b_a.reshape(1, gw), norm_w.reshape(1, G_DV))


def _merge_kernel(ym_ref, yd_ref, yg_ref, gm_ref, gd_ref, gg_ref, w_hbm, o_ref, stage, wb, sem, *, tn, layer, nj):
    tm = o_ref.shape[0]

    y_refs = (ym_ref, yd_ref, yg_ref)
    g_refs = (gm_ref, gd_ref, gg_ref)
    gw = _group_width(tn)
    tiles = [dict(layer=g, col0=0, tn=tn, nj=nj, k_minor=False) for g in range(N_BRANCH)]

    def epilogue(r0, tr, raw):
        for n, g0 in enumerate(range(0, tn, gw)):
            acc = _sigmoid(g_refs[0][r0:r0 + tr, g0:g0 + gw].astype(F32)) * raw[n][0]
            for g in range(1, N_BRANCH):
                acc = acc + _sigmoid(g_refs[g][r0:r0 + tr, g0:g0 + gw].astype(F32)) * raw[n][g]
            o_ref[r0:r0 + tr, g0:g0 + gw] = acc.astype(o_ref.dtype)

    def step(first):
        if first:
            for g in range(N_BRANCH):
                _weight_tile_arrive(w_hbm.at[layer], stage.at[g], sem.at[g], **tiles[g])

        def products(r0, tr):
            out = []
            for g0 in range(0, tn, gw):
                if first and r0 == 0:
                    wb[:, :, g0:g0 + gw] = stage[:, :, g0:g0 + gw].astype(BF16)
                out.append([_dot(y_refs[g][r0:r0 + tr, :], wb[g, :, g0:g0 + gw]) for g in range(N_BRANCH)])
            return out

        _pipelined_pieces(tm, MERGE_HALVINGS, products, epilogue)
        if first:
            for g in range(N_BRANCH):
                _weight_tile_prefetch(w_hbm.at[layer], stage.at[g], sem.at[g], **tiles[g])

    first_row_tile = pl.program_id(1) == 0
    pl.when(first_row_tile)(functools.partial(step, True))
    pl.when(jnp.logical_not(first_row_tile))(functools.partial(step, False))


def _merge(y_m, y_d, y_g, w_branch, layer, merge_pre, d_model):
    t = y_m.shape[0]
    tm = _tile(t, ROW_TILE)
    tn = _tile(d_model, COL_TILE_FFN)
    nj = d_model // tn
    yspec = pl.BlockSpec((tm, BRANCH_W), lambda j, i: (i, 0))
    gspec = lambda g: pl.BlockSpec((tm, tn), lambda j, i: (i, g * nj + j))
    return pl.pallas_call(
        functools.partial(_merge_kernel, tn=tn, layer=layer, nj=nj),
        grid=(nj, t // tm),
        in_specs=[yspec, yspec, yspec, gspec(0), gspec(1), gspec(2), pl.BlockSpec(memory_space=pl.ANY)],
        out_specs=pl.BlockSpec((tm, tn), lambda j, i: (i, j)),
        out_shape=jax.ShapeDtypeStruct((t, d_model), BF16),
        scratch_shapes=[pltpu.VMEM((N_BRANCH, BRANCH_W, tn), F32), pltpu.VMEM((N_BRANCH, BRANCH_W, tn), BF16),
                        pltpu.SemaphoreType.DMA((N_BRANCH,))],
        compiler_params=_cparams("arbitrary", "arbitrary"),
        name="merge",
    )(y_m, y_d, y_g, merge_pre, merge_pre, merge_pre, w_branch)


def _ffn_up_kernel(h_ref, w_hbm, cg_ref, cu_ref, o_ref, stage, wb, sem, tail_g, tail_u, *, tn, d_ff, rows_per_seq, layer, nj):
    i = pl.program_id(1)
    tm = h_ref.shape[0]

    @pl.when((i * tm) % rows_per_seq == 0)
    def _():
        tail_g[...] = jnp.zeros_like(tail_g)
        tail_u[...] = jnp.zeros_like(tail_u)

    gw = _group_width(tn)
    tiles = [dict(layer=layer, col0=col0, tn=tn, nj=nj, k_minor=False) for col0 in (0, d_ff)]

    def epilogue(r0, tr, raw):
        for n, g0 in enumerate(range(0, tn, gw)):
            gate = _causal_conv(raw[n][0], tail_g, g0, cg_ref, FFN_CONV)
            up = _causal_conv(raw[n][1], tail_u, g0, cu_ref, FFN_CONV)
            o_ref[r0:r0 + tr, g0:g0 + gw] = (_silu(gate) * up).astype(o_ref.dtype)

    def step(first):
        if first:
            for m in range(2):
                _weight_tile_arrive(w_hbm, stage.at[m], sem.at[m], **tiles[m])

        def products(r0, tr):
            h = h_ref[r0:r0 + tr, :]
            out = []
            for g0 in range(0, tn, gw):
                if first and r0 == 0:
                    wb[:, :, g0:g0 + gw] = stage[:, :, g0:g0 + gw].astype(BF16)
                out.append((_dot(h, wb[0, :, g0:g0 + gw]), _dot(h, wb[1, :, g0:g0 + gw])))
            return out

        _pipelined_pieces(tm, FFN_HALVINGS, products, epilogue)
        if first:
            for m in range(2):
                _weight_tile_prefetch(w_hbm, stage.at[m], sem.at[m], **tiles[m])

    pl.when(i == 0)(functools.partial(step, True))
    pl.when(i != 0)(functools.partial(step, False))


def _ffn_up(h, w_up, conv_w, layer, seq):
    t, k = h.shape
    d_ff = w_up.shape[-1] // 2
    tm = _tile(seq, ROW_TILE)
    tn = _tile(d_ff, COL_TILE_FFN)
    nj = d_ff // tn
    return pl.pallas_call(
        functools.partial(_ffn_up_kernel, tn=tn, d_ff=d_ff, rows_per_seq=seq, layer=layer, nj=nj),
        grid=(nj, t // tm),
        in_specs=[
            pl.BlockSpec((tm, k), lambda j, i: (i, 0)),
            pl.BlockSpec(memory_space=pl.ANY),
            pl.BlockSpec((None, FFN_CONV, tn), lambda j, i: (layer, 0, j)),
            pl.BlockSpec((None, FFN_CONV, tn), lambda j, i: (layer, 0, nj + j)),
        ],
        out_specs=pl.BlockSpec((tm, tn), lambda j, i: (i, j)),
        out_shape=jax.ShapeDtypeStruct((t, d_ff), BF16),
        scratch_shapes=[pltpu.VMEM((2, k, tn), F32), pltpu.VMEM((2, k, tn), BF16), pltpu.SemaphoreType.DMA((2,)),
                        pltpu.VMEM((SUBLANE, tn), F32), pltpu.VMEM((SUBLANE, tn), F32)],
        compiler_params=_cparams("arbitrary", "arbitrary"),
        name="ffn_up",
    )(h, w_up, conv_w, conv_w)


def kernel(x, norm_mix_pre, norm_mix_post, norm_ffn_pre, norm_ffn_post, w_in, mlstm_gate_b, mlstm_norm, dn_conv, dn_a_log, dn_dt_bias, dn_norm, gla_w_a2, gla_b_a, gla_norm, w_branch, w_out, w_ffn_up, ffn_conv, w_ffn_down):
    bsz, seq, d_model = x.shape
    depth = w_in.shape[0]
    t = bsz * seq
    x2 = x.reshape(t, d_model)
    wt = jnp.transpose(w_in, (2, 0, 1))
    proj = functools.partial(_wsmm, k_minor=True)
    dense = functools.partial(_wsmm, k_minor=False)
    h = _norm_cast(x2, norm_mix_pre[0])
    for l in range(depth):
        proj_m = proj(h, wt, l, OFF_M, M_W, BF16, tn=COL_TILE_SLAB, name="proj_m")
        d_qkv = proj(h, wt, l, OFF_D, DQKV_W, BF16, tn=COL_TILE_SLAB, name="proj_dqkv",
                     conv=dn_conv[l], seq=seq, l2_cols=2 * DN_HEADS * DN_DK)
        d_z = proj(h, wt, l, OFF_DZ, DZ_W, BF16, tn=COL_TILE_SLAB, name="proj_dz")
        proj_g = proj(h, wt, l, OFF_G, G_W, BF16, tn=COL_TILE_SLAB, name="proj_g")
        merge_pre = proj(h, wt, l, OFF_MERGE, N_BRANCH * d_model, BF16, tn=COL_TILE_WIDE, name="proj_merge")
        gates, gates_t = _gates(h, wt, l)
        y_m = _mlstm(proj_m, gates_t, mlstm_gate_b[l], mlstm_norm[l], bsz, seq)
        y_d = _gdn(d_qkv, d_z, gates, gates_t, dn_a_log[l], dn_dt_bias[l], dn_norm[l], bsz, seq)
        y_g = _gla(proj_g, gates, gla_w_a2[l], gla_b_a[l], gla_norm[l], bsz, seq)
        merged = _merge(y_m, y_d, y_g, w_branch, l, merge_pre, d_model)
        mix = dense(merged, w_out, l, 0, d_model, BF16, tn=COL_TILE_WIDE, name="out_proj")
        x2, h = _resid_norm(x2, mix, norm_mix_post[l], norm_ffn_pre[l])
        act = _ffn_up(h, w_ffn_up, ffn_conv, l, seq)
        down = dense(act, w_ffn_down, l, 0, d_model, BF16, tn=COL_TILE_FFN, name="ffn_down")
        x2, h = _resid_norm(x2, down, norm_ffn_post[l], norm_mix_pre[l + 1] if l + 1 < depth else None)
    return x2.reshape(bsz, seq, d_model)
```
